```python
import jax, jax.numpy as jnp
from jax import lax
import numpy as np

D_MODEL = 1024
BATCH = 4
SEQ = 8192
DEPTH = 2
DEC_BATCH = 32
DEC_SEQ = 8
PAST_LEN = 16384
PAGE_SIZE = 128

N_HEADS = 16
HEAD_DIM = D_MODEL // N_HEADS
D_FF = 2816
CONV_WIDTH = 3
Q_BLOCK = 128
N_MIXERS = 2
N_ATTN_LAYERS = (DEPTH + 1) // 2
N_CONV_LAYERS = DEPTH // 2
RMS_EPS = 1e-6
POOL_NUM = 5
POOL_DEN = 4
BIAS_LO = -9.0
BIAS_HI = -3.0

kernel_name = "stickbreak_shortconv_macaron_step"


def rmsnorm(x, g):
    xf = x.astype(jnp.float32)
    y = xf * lax.rsqrt(jnp.mean(xf * xf, axis=-1, keepdims=True) + RMS_EPS)
    return (y * g.astype(jnp.float32)).astype(x.dtype)


def swiglu(x, w_up, w_down):
    gate, up = jnp.split(x @ w_up, 2, axis=-1)
    return (jax.nn.silu(gate) * up) @ w_down


def half_ffn(x, g_pre, g_post, w_up, w_down):
    return x + 0.5 * rmsnorm(swiglu(rmsnorm(x, g_pre), w_up, w_down), g_post)


def stick_breaking(q, k, v, bias, q_pos, k_pos):
    z = jnp.einsum("bqhd,bkhd->bhqk", q, k).astype(jnp.float32) * (HEAD_DIM ** -0.5)
    z = z + bias.astype(jnp.float32)[None, :, None, None]
    causal = k_pos[None, :] < q_pos[:, None]
    log_keep = jnp.where(causal, jax.nn.log_sigmoid(-z), 0.0)
    later = lax.cumsum(log_keep, axis=3, reverse=True) - log_keep
    log_w = jnp.where(causal, jax.nn.log_sigmoid(z) + later, -jnp.inf)
    w = jnp.exp(log_w)
    return jnp.einsum("bhqk,bkhd->bqhd", w.astype(v.dtype), v)


def attn_project(h, w_qkv):
    n, t, _ = h.shape
    q, k, v = jnp.split(h @ w_qkv, 3, axis=-1)
    shape = (n, t, N_HEADS, HEAD_DIM)
    return q.reshape(shape), k.reshape(shape), v.reshape(shape)


def prompt_attention(q, k, v, bias):
    n, s = q.shape[0], q.shape[1]
    nb = s // Q_BLOCK
    qb = q.reshape(n, nb, Q_BLOCK, N_HEADS, HEAD_DIM).transpose(1, 0, 2, 3, 4)
    k_pos = jnp.arange(s)

    def one_block(args):
        q_blk, b = args
        q_pos = b * Q_BLOCK + jnp.arange(Q_BLOCK)
        return stick_breaking(q_blk, k, v, bias, q_pos, k_pos)

    o = lax.map(one_block, (qb, jnp.arange(nb)))
    return o.transpose(1, 0, 2, 3, 4).reshape(n, s, N_HEADS * HEAD_DIM)


def sample_attention(q, k_new, v_new, bias, k_pool, v_pool, page_table):
    n, t = q.shape[0], q.shape[1]
    past = page_table.shape[1] * k_pool.shape[1]
    k_past = k_pool[page_table].reshape(n, past, N_HEADS, HEAD_DIM).astype(k_new.dtype)
    v_past = v_pool[page_table].reshape(n, past, N_HEADS, HEAD_DIM).astype(v_new.dtype)
    k_all = jnp.concatenate([k_past, k_new], axis=1)
    v_all = jnp.concatenate([v_past, v_new], axis=1)
    k_pos = jnp.arange(past + t)
    q_pos = past + jnp.arange(t)
    o = stick_breaking(q, k_all, v_all, bias, q_pos, k_pos)
    return o.reshape(n, t, N_HEADS * HEAD_DIM)


def short_conv(h, conv_state, w_in, conv_w, w_out):
    b_gate, c_gate, val = jnp.split(h @ w_in, 3, axis=-1)
    u = c_gate * val
    t = u.shape[1]
    u_full = jnp.concatenate([conv_state.astype(u.dtype), u], axis=1)
    conv = conv_w[0] * u_full[:, 0:t]
    for i in range(1, CONV_WIDTH):
        conv = conv + conv_w[i] * u_full[:, i:i + t]
    y = (b_gate * conv) @ w_out
    return y, u_full[:, -(CONV_WIDTH - 1):]


def setup_inputs(seed: int = 0) -> dict:
    key = jax.random.key(seed)
    ks = jax.random.split(key, 16)
    n_pages = PAST_LEN // PAGE_SIZE
    n_pool = (DEC_BATCH * n_pages * POOL_NUM) // POOL_DEN
    f32 = jnp.float32
    x_prompt = jax.random.normal(ks[0], (BATCH, SEQ, D_MODEL), f32)
    x_sample = jax.random.normal(ks[1], (DEC_BATCH, DEC_SEQ, D_MODEL), f32)
    cache_k = jax.random.normal(ks[2], (N_ATTN_LAYERS, n_pool, PAGE_SIZE, N_HEADS, HEAD_DIM), f32)
    cache_v = jax.random.normal(ks[3], (N_ATTN_LAYERS, n_pool, PAGE_SIZE, N_HEADS, HEAD_DIM), f32)
    state_conv = jax.random.normal(ks[4], (N_CONV_LAYERS, DEC_BATCH, CONV_WIDTH - 1, D_MODEL), f32)
    page_table = jax.random.permutation(ks[5], n_pool)[: DEC_BATCH * n_pages].reshape(
        DEC_BATCH, n_pages).astype(jnp.int32)
    norm_gain = 1.0 + 0.02 * jax.random.normal(ks[6], (DEPTH, 6, D_MODEL), f32)
    w_ffn_up = jax.random.normal(ks[7], (DEPTH, 2, D_MODEL, 2 * D_FF), f32) * D_MODEL ** -0.5
    w_ffn_down = jax.random.normal(ks[8], (DEPTH, 2, D_FF, D_MODEL), f32) * D_FF ** -0.5
    w_attn_qkv = jax.random.normal(ks[9], (N_ATTN_LAYERS, D_MODEL, 3 * D_MODEL), f32) * D_MODEL ** -0.5
    w_attn_out = jax.random.normal(ks[10], (N_ATTN_LAYERS, D_MODEL, D_MODEL), f32) * D_MODEL ** -0.5
    attn_logit_bias = jax.random.uniform(ks[14], (N_ATTN_LAYERS, N_HEADS), f32, BIAS_LO, BIAS_HI)
    w_conv_in = jax.random.normal(ks[11], (N_CONV_LAYERS, D_MODEL, 3 * D_MODEL), f32) * D_MODEL ** -0.5
    conv_w = jax.random.normal(ks[12], (N_CONV_LAYERS, CONV_WIDTH, D_MODEL), f32) * CONV_WIDTH ** -0.5
    w_conv_out = jax.random.normal(ks[13], (N_CONV_LAYERS, D_MODEL, D_MODEL), f32) * D_MODEL ** -0.5
    return {"x_prompt": x_prompt, "x_sample": x_sample, "cache_k": cache_k, "cache_v": cache_v,
            "state_conv": state_conv, "page_table": page_table, "norm_gain": norm_gain,
            "w_ffn_up": w_ffn_up, "w_ffn_down": w_ffn_down, "w_attn_qkv": w_attn_qkv,
            "w_attn_out": w_attn_out, "attn_logit_bias": attn_logit_bias, "w_conv_in": w_conv_in,
            "conv_w": conv_w, "w_conv_out": w_conv_out}


def reference(x_prompt, x_sample, cache_k, cache_v, state_conv, page_table, norm_gain,
              w_ffn_up, w_ffn_down, w_attn_qkv, w_attn_out, attn_logit_bias, w_conv_in, conv_w, w_conv_out):
    xp, xs = x_prompt, x_sample
    k_p_list, v_p_list, c_p_list = [], [], []
    k_s_list, v_s_list, c_s_list = [], [], []
    for i in range(DEPTH):
        g = norm_gain[i]
        xp = half_ffn(xp, g[0], g[1], w_ffn_up[i, 0], w_ffn_down[i, 0])
        xs = half_ffn(xs, g[0], g[1], w_ffn_up[i, 0], w_ffn_down[i, 0])
        hp, hs = rmsnorm(xp, g[2]), rmsnorm(xs, g[2])
        if i % N_MIXERS == 0:
            a = i // N_MIXERS
            qp, kp, vp = attn_project(hp, w_attn_qkv[a])
            qs, ks_, vs = attn_project(hs, w_attn_qkv[a])
            mp = prompt_attention(qp, kp, vp, attn_logit_bias[a]) @ w_attn_out[a]
            ms = sample_attention(qs, ks_, vs, attn_logit_bias[a], cache_k[a], cache_v[a],
                                  page_table) @ w_attn_out[a]
            k_p_list.append(kp); v_p_list.append(vp)
            k_s_list.append(ks_); v_s_list.append(vs)
        else:
            c = i // N_MIXERS
            zero_state = jnp.zeros((xp.shape[0], CONV_WIDTH - 1, D_MODEL), xp.dtype)
            mp, cp_new = short_conv(hp, zero_state, w_conv_in[c], conv_w[c], w_conv_out[c])
            ms, cs_new = short_conv(hs, state_conv[c], w_conv_in[c], conv_w[c], w_conv_out[c])
            c_p_list.append(cp_new); c_s_list.append(cs_new)
        xp = xp + rmsnorm(mp, g[3])
        xs = xs + rmsnorm(ms, g[3])
        xp = half_ffn(xp, g[4], g[5], w_ffn_up[i, 1], w_ffn_down[i, 1])
        xs = half_ffn(xs, g[4], g[5], w_ffn_up[i, 1], w_ffn_down[i, 1])
    k_prompt_new = jnp.stack(k_p_list)
    v_prompt_new = jnp.stack(v_p_list)
    conv_prompt_new = jnp.stack(c_p_list)
    k_sample_new = jnp.stack(k_s_list)
    v_sample_new = jnp.stack(v_s_list)
    conv_sample_new = jnp.stack(c_s_list)
    return (xp, xs, k_prompt_new, v_prompt_new, conv_prompt_new, k_sample_new, v_sample_new, conv_sample_new)
```

```python
import functools

import jax
import jax.numpy as jnp
from jax import lax
from jax.experimental import pallas as pl
from jax.experimental.pallas import tpu as pltpu

F32 = jnp.float32
BF16 = jnp.bfloat16

RMS_EPS = 1e-6
HEAD_DIM = 64
LANES = 128
FF_CHUNK = 256
VMEM_LIMIT = 56 * 1024 * 1024


def _params(sem):
    return pltpu.CompilerParams(dimension_semantics=sem, vmem_limit_bytes=VMEM_LIMIT)


def _resident(shape):
    nd = len(shape)
    return pl.BlockSpec(shape, lambda *_: (0,) * nd, pipeline_mode=pl.Buffered(1))


def _rms(x, g):
    ms = jnp.mean(x * x, axis=-1, keepdims=True)
    return (x * lax.rsqrt(ms + RMS_EPS)) * g


def _softplus(z):
    return jnp.maximum(z, 0.0) + jnp.log1p(jnp.exp(-jnp.abs(z)))


def _ffn_kernel(x_ref, g_ref, wup_ref, wdn_ref, o_ref, *, n_chunks):
    x = x_ref[...]
    h = _rms(x, g_ref[0:1, :]).astype(BF16)
    acc = jnp.zeros(x.shape, F32)
    for j in range(n_chunks):
        gate = jnp.dot(h, wup_ref[j], preferred_element_type=F32)
        up = jnp.dot(h, wup_ref[n_chunks + j], preferred_element_type=F32)
        a = ((gate * jax.nn.sigmoid(gate)) * up).astype(BF16)
        acc = acc + jnp.dot(a, wdn_ref[j], preferred_element_type=F32)
    o_ref[...] = x + 0.5 * _rms(acc, g_ref[1:2, :])


def _ffn_half(x, g2, wup_r, wdn_r, tm):
    n, d = x.shape
    n_chunks = wdn_r.shape[0]
    return pl.pallas_call(
        functools.partial(_ffn_kernel, n_chunks=n_chunks),
        grid=(n // tm,),
        in_specs=[
            pl.BlockSpec((tm, d), lambda i: (i, 0)),
            _resident(g2.shape),
            _resident(wup_r.shape),
            _resident(wdn_r.shape),
        ],
        out_specs=pl.BlockSpec((tm, d), lambda i: (i, 0)),
        out_shape=jax.ShapeDtypeStruct((n, d), F32),
        compiler_params=_params(("parallel",)),
        name="ffn_half",
    )(x, g2, wup_r, wdn_r)


def _attn_pre_kernel(x_ref, g_ref, w_ref, q_ref, kf_ref, vf_ref, kb_ref, vb_ref):
    h = _rms(x_ref[...], g_ref[...]).astype(BF16)
    q = jnp.dot(h, w_ref[0], preferred_element_type=F32)
    q_ref[...] = (q * (HEAD_DIM ** -0.5)).astype(BF16)
    k = jnp.dot(h, w_ref[1], preferred_element_type=F32)
    kf_ref[...] = k
    kb_ref[...] = k.astype(BF16)
    v = jnp.dot(h, w_ref[2], preferred_element_type=F32)
    vf_ref[...] = v
    vb_ref[...] = v.astype(BF16)


def _attn_pre(x, g, w3, tm):
    n, d = x.shape
    row = pl.BlockSpec((tm, d), lambda i: (i, 0))
    return pl.pallas_call(
        _attn_pre_kernel,
        grid=(n // tm,),
        in_specs=[row, _resident(g.shape), _resident(w3.shape)],
        out_specs=[row] * 5,
        out_shape=[jax.ShapeDtypeStruct((n, d), dt) for dt in (BF16, F32, F32, BF16, BF16)],
        compiler_params=_params(("parallel",)),
        name="attn_pre",
    )(x, g, w3)


def _mix_post_kernel(o_ref, x_ref, g_ref, w_ref, y_ref):
    m = jnp.dot(o_ref[...], w_ref[...], preferred_element_type=F32)
    y_ref[...] = x_ref[...] + _rms(m, g_ref[...])


def _mix_post(o, x, g, w, tm):
    n, d = x.shape
    row = pl.BlockSpec((tm, d), lambda i: (i, 0))
    return pl.pallas_call(
        _mix_post_kernel,
        grid=(n // tm,),
        in_specs=[row, row, _resident(g.shape), _resident(w.shape)],
        out_specs=row,
        out_shape=jax.ShapeDtypeStruct((n, d), F32),
        compiler_params=_params(("parallel",)),
        name="mix_post",
    )(o, x, g, w)


def _later_matrix(tk):
    j = lax.broadcasted_iota(jnp.int32, (tk, tk), 0)
    s = lax.broadcasted_iota(jnp.int32, (tk, tk), 1)
    return jnp.where(j > s, 1.0, 0.0).astype(BF16)


def _sb_weights(z, c, later_mat, causal):
    sp = _softplus(z)
    if causal is not None:
        sp = jnp.where(causal, sp, 0.0)
    later = jnp.dot(sp.astype(BF16), later_mat, preferred_element_type=F32)
    w = jnp.exp(z - sp - later - c)
    if causal is not None:
        w = jnp.where(causal, w, 0.0)
    return w, c + jnp.sum(sp, axis=-1, keepdims=True)


def _prompt_attn_kernel(bias_ref, q_ref, k_ref, v_ref, o_ref, *, tq):
    p = pl.program_id(1)
    i = pl.program_id(2)
    q = q_ref[...]
    lane = lax.broadcasted_iota(jnp.int32, (tq, LANES), 1)
    zero = jnp.zeros_like(q)
    qs = jnp.concatenate([jnp.where(lane < HEAD_DIM, q, zero), jnp.where(lane >= HEAD_DIM, q, zero)], axis=0)
    row = lax.broadcasted_iota(jnp.int32, (2 * tq, 1), 0)
    bcol = jnp.where(row < tq, bias_ref[2 * p], bias_ref[2 * p + 1])
    later_mat = _later_matrix(tq)

    def tile(kt, c, acc, causal):
        start = pl.multiple_of(kt * tq, tq)
        k2 = k_ref[pl.ds(start, tq), :]
        v2 = v_ref[pl.ds(start, tq), :]
        z = lax.dot_general(qs, k2, (((1,), (1,)), ((), ())), preferred_element_type=F32) + bcol
        w, c = _sb_weights(z, c, later_mat, causal)
        return c, acc + jnp.dot(w.astype(BF16), v2, preferred_element_type=F32)

    t = lax.broadcasted_iota(jnp.int32, (2 * tq, tq), 0)
    t = jnp.where(t >= tq, t - tq, t)
    s = lax.broadcasted_iota(jnp.int32, (2 * tq, tq), 1)
    c0 = jnp.zeros((2 * tq, 1), F32)
    acc0 = jnp.zeros((2 * tq, LANES), F32)
    c, acc = tile(i, c0, acc0, s < t)
    c, acc = lax.fori_loop(0, i, lambda n, ca: tile(i - 1 - n, ca[0], ca[1], None), (c, acc))
    o_ref[...] = jnp.where(lane < HEAD_DIM, acc[:tq], acc[tq:]).astype(o_ref.dtype)


def _prompt_attention(q, k, v, bias, batch, seq, tq):
    n, d = q.shape
    n_pairs = d // LANES
    nq = seq // tq
    q_spec = pl.BlockSpec((tq, LANES), lambda b, p, i: (b * nq + i, p))
    kv_spec = pl.BlockSpec((seq, LANES), lambda b, p, i: (b, p))
    return pl.pallas_call(
        functools.partial(_prompt_attn_kernel, tq=tq),
        grid=(batch, n_pairs, nq),
        in_specs=[pl.BlockSpec(memory_space=pltpu.SMEM), q_spec, kv_spec, kv_spec],
        out_specs=q_spec,
        out_shape=jax.ShapeDtypeStruct((n, d), BF16),
        compiler_params=_params(("parallel", "parallel", "arbitrary")),
        name="prompt_attn",
    )(bias, q, k, v)


def _sample_attn_kernel(pt_ref, qbd_ref, bcol_ref, kn_ref, vn_ref, *rest, n_heads, t_new, pps):
    del pt_ref
    k_refs = rest[:pps]
    v_refs = rest[pps:2 * pps]
    o_ref, acc_ref, c_ref = rest[2 * pps:]
    j = pl.program_id(1)
    rows = n_heads * t_new
    page = kn_ref.shape[0]
    qbd = qbd_ref[...]
    bcol = bcol_ref[...]
    later_mat = _later_matrix(page)

    def tile(kb, vb, causal):
        z = lax.dot_general(qbd, kb, (((1,), (1,)), ((), ())), preferred_element_type=F32) + bcol
        w, c = _sb_weights(z, c_ref[...], later_mat, causal)
        c_ref[...] = c
        acc_ref[...] += jnp.dot(w.astype(BF16), vb, preferred_element_type=F32)

    @pl.when(j == 0)
    def _():
        acc_ref[...] = jnp.zeros_like(acc_ref)
        c_ref[...] = jnp.zeros_like(c_ref)
        t = lax.broadcasted_iota(jnp.int32, (rows, page), 0) % t_new
        s = lax.broadcasted_iota(jnp.int32, (rows, page), 1)
        tile(kn_ref[...], vn_ref[...], s < t)

    for r in range(pps):
        tile(k_refs[r][...].astype(BF16), v_refs[r][...].astype(BF16), None)

    @pl.when(j == pl.num_programs(1) - 1)
    def _():
        d = acc_ref.shape[1]
        head_of_lane = lax.broadcasted_iota(jnp.int32, (t_new, d), 1) // HEAD_DIM
        o = jnp.zeros((t_new, d), F32)
        for h in range(n_heads):
            o = jnp.where(head_of_lane == h, acc_ref[h * t_new:(h + 1) * t_new, :], o)
        o_ref[...] = o.astype(o_ref.dtype)


def _sample_attention(q, k_new, v_new, bias, k_pool, v_pool, page_table, layer, dec_batch, t_new, pps):
    n, d = q.shape
    n_heads = d // HEAD_DIM
    rows = n_heads * t_new
    page = k_pool.shape[2]
    n_pages = page_table.shape[1]
    assert n_pages % pps == 0 and t_new <= page
    q4 = q.reshape(dec_batch, t_new, n_heads, HEAD_DIM)
    eye = jnp.eye(n_heads, dtype=q.dtype)
    qbd = jnp.einsum("nthd,hg->nhtgd", q4, eye).reshape(dec_batch, rows, d)
    bcol = jnp.repeat(bias.astype(F32), t_new).reshape(rows, 1)
    pad = ((0, 0), (0, page - t_new), (0, 0))
    kn = jnp.pad(k_new.reshape(dec_batch, t_new, d), pad)
    vn = jnp.pad(v_new.reshape(dec_batch, t_new, d), pad)

    per_seq = lambda shape: pl.BlockSpec((None,) + shape, lambda b, j, pt: (b, 0, 0))

    def page_spec(r):
        return pl.BlockSpec((None, None, page, d),
                            lambda b, j, pt: (layer, pt[b, n_pages - 1 - (j * pps + r)], 0, 0))

    grid_spec = pltpu.PrefetchScalarGridSpec(
        num_scalar_prefetch=1,
        grid=(dec_batch, n_pages // pps),
        in_specs=[per_seq((rows, d)), pl.BlockSpec((rows, 1), lambda b, j, pt: (0, 0)),
                  per_seq((page, d)), per_seq((page, d))]
        + [page_spec(r) for r in range(pps)] * 2,
        out_specs=per_seq((t_new, d)),
        scratch_shapes=[pltpu.VMEM((rows, d), F32), pltpu.VMEM((rows, 1), F32)],
    )
    o = pl.pallas_call(
        functools.partial(_sample_attn_kernel, n_heads=n_heads, t_new=t_new, pps=pps),
        grid_spec=grid_spec,
        out_shape=jax.ShapeDtypeStruct((dec_batch, t_new, d), BF16),
        compiler_params=_params(("parallel", "arbitrary")),
        name="sample_attn",
    )(page_table, qbd, bcol, kn, vn, *([k_pool] * pps), *([v_pool] * pps))
    return o.reshape(n, d)


def _conv_pre_kernel(x_ref, g_ref, w_ref, u_ref, b_ref):
    h = _rms(x_ref[...], g_ref[...]).astype(BF16)
    b_ref[...] = jnp.dot(h, w_ref[0], preferred_element_type=F32)
    c_gate = jnp.dot(h, w_ref[1], preferred_element_type=F32)
    val = jnp.dot(h, w_ref[2], preferred_element_type=F32)
    u_ref[...] = c_gate * val


def _conv_pre(x, g, w3, tm):
    n, d = x.shape
    row = pl.BlockSpec((tm, d), lambda i: (i, 0))
    return pl.pallas_call(
        _conv_pre_kernel,
        grid=(n // tm,),
        in_specs=[row, _resident(g.shape), _resident(w3.shape)],
        out_specs=[row, row],
        out_shape=[jax.ShapeDtypeStruct((n, d), F32)] * 2,
        compiler_params=_params(("parallel",)),
        name="conv_pre",
    )(x, g, w3)


def _conv_post_kernel(u_ref, halo_ref, b_ref, x_ref, cw_ref, g_ref, w_ref, y_ref, *, tiles_per_seq, zero_state):
    u = u_ref[...]
    tm = u.shape[0]
    prev1 = halo_ref[7:8, :]
    prev2 = halo_ref[6:7, :]
    if zero_state:
        first = (pl.program_id(0) % tiles_per_seq) == 0
        prev1 = jnp.where(first, 0.0, prev1)
        prev2 = jnp.where(first, 0.0, prev2)
    row = lax.broadcasted_iota(jnp.int32, u.shape, 0)
    u1 = jnp.where(row == 0, prev1, pltpu.roll(u, 1, 0))
    u2 = jnp.where(row == 0, prev2, jnp.where(row == 1, prev1, pltpu.roll(u, 2, 0)))
    conv = cw_ref[0:1, :] * u2 + cw_ref[1:2, :] * u1 + cw_ref[2:3, :] * u
    y = (b_ref[...] * conv).astype(BF16)
    m = jnp.dot(y, w_ref[...], preferred_element_type=F32)
    y_ref[...] = x_ref[...] + _rms(m, g_ref[...])
    del tm


def _conv_post(u, halo_src, b, x, conv_w, g, w, tm, seq, zero_state):
    n, d = x.shape
    assert seq % tm == 0 and tm % 8 == 0
    row = pl.BlockSpec((tm, d), lambda i: (i, 0))
    if zero_state:
        halo = pl.BlockSpec((8, d), lambda i: (jnp.maximum(i * (tm // 8) - 1, 0), 0))
    else:
        halo = pl.BlockSpec((8, d), lambda i: (i, 0))
    return pl.pallas_call(
        functools.partial(_conv_post_kernel, tiles_per_seq=seq // tm, zero_state=zero_state),
        grid=(n // tm,),
        in_specs=[row, halo, row, row, _resident(conv_w.shape), _resident(g.shape), _resident(w.shape)],
        out_specs=row,
        out_shape=jax.ShapeDtypeStruct((n, d), F32),
        compiler_params=_params(("parallel",)),
        name="conv_post",
    )(u, halo_src, b, x, conv_w, g, w)


def _row_tile(n, target):
    tm = min(n, target)
    assert n % tm == 0
    return tm


def kernel(x_prompt, x_sample, cache_k, cache_v, state_conv, page_table, norm_gain,
           w_ffn_up, w_ffn_down, w_attn_qkv, w_attn_out, attn_logit_bias, w_conv_in, conv_w, w_conv_out):
    batch, seq, d = x_prompt.shape
    dec_batch, t_new, _ = x_sample.shape
    depth = norm_gain.shape[0]
    d_ff = w_ffn_down.shape[2]
    n_heads = attn_logit_bias.shape[1]
    assert d == n_heads * HEAD_DIM and d % LANES == 0 and d_ff % FF_CHUNK == 0
    n_chunks = d_ff // FF_CHUNK
    n_mixers = 2

    xp = x_prompt.reshape(batch * seq, d)
    xs = x_sample.reshape(dec_batch * t_new, d)
    tm_p = _row_tile(batch * seq, 512)
    tm_s = _row_tile(dec_batch * t_new, 256)
    tq = _row_tile(seq, 256)
    assert seq % tm_p == 0

    wup_r = w_ffn_up.astype(BF16).reshape(depth, 2, d, 2 * n_chunks, FF_CHUNK).transpose(0, 1, 3, 2, 4)
    wdn_r = w_ffn_down.astype(BF16).reshape(depth, 2, n_chunks, FF_CHUNK, d)
    split3 = lambda w: w.astype(BF16).reshape(w.shape[0], d, 3, d).transpose(0, 2, 1, 3)
    wqkv_r = split3(w_attn_qkv)
    wcin_r = split3(w_conv_in)
    wao = w_attn_out.astype(BF16)
    wco = w_conv_out.astype(BF16)
    pool_k = cache_k.reshape(cache_k.shape[:3] + (d,))
    pool_v = cache_v.reshape(cache_v.shape[:3] + (d,))

    outs = {name: [] for name in ("kp", "vp", "cp", "ks", "vs", "cs")}
    for i in range(depth):
        g = norm_gain[i]
        ffn = lambda x, half, tm: _ffn_half(x, g[4 * half:4 * half + 2], wup_r[i, half], wdn_r[i, half], tm)
        xp = ffn(xp, 0, tm_p)
        xs = ffn(xs, 0, tm_s)
        g_pre, g_post = g[2:3], g[3:4]
        if i % n_mixers == 0:
            a = i // n_mixers
            bias = attn_logit_bias[a].astype(F32)
            qp, kpf, vpf, kpb, vpb = _attn_pre(xp, g_pre, wqkv_r[a], tm_p)
            qs, ksf, vsf, ksb, vsb = _attn_pre(xs, g_pre, wqkv_r[a], tm_s)
            op = _prompt_attention(qp, kpb, vpb, bias, batch, seq, tq)
            os_ = _sample_attention(qs, ksb, vsb, bias, pool_k, pool_v, page_table, a, dec_batch, t_new, pps=4)
            xp = _mix_post(op, xp, g_post, wao[a], tm_p)
            xs = _mix_post(os_, xs, g_post, wao[a], tm_s)
            kv_p = (batch, seq, n_heads, HEAD_DIM)
            kv_s = (dec_batch, t_new, n_heads, HEAD_DIM)
            outs["kp"].append(kpf.reshape(kv_p)); outs["vp"].append(vpf.reshape(kv_p))
            outs["ks"].append(ksf.reshape(kv_s)); outs["vs"].append(vsf.reshape(kv_s))
        else:
            c = i // n_mixers
            n_state = state_conv.shape[2]
            assert n_state == conv_w.shape[1] - 1 == 2 and seq >= n_state and t_new >= n_state and t_new == 8
            up_, bp = _conv_pre(xp, g_pre, wcin_r[c], tm_p)
            us_, bs = _conv_pre(xs, g_pre, wcin_r[c], tm_s)
            xp = _conv_post(up_, up_, bp, xp, conv_w[c], g_post, wco[c], tm_p, seq, zero_state=True)
            halo_s = jnp.pad(state_conv[c], ((0, 0), (8 - n_state, 0), (0, 0))).reshape(dec_batch * 8, d)
            xs = _conv_post(us_, halo_s, bs, xs, conv_w[c], g_post, wco[c], t_new, t_new, zero_state=False)
            outs["cp"].append(up_.reshape(batch, seq, d)[:, seq - n_state:])
            outs["cs"].append(us_.reshape(dec_batch, t_new, d)[:, t_new - n_state:])
        xp = ffn(xp, 1, tm_p)
        xs = ffn(xs, 1, tm_s)

    return (xp.reshape(batch, seq, d), xs.reshape(dec_batch, t_new, d),
            jnp.stack(outs["kp"]), jnp.stack(outs["vp"]), jnp.stack(outs["cp"]),
            jnp.stack(outs["ks"]), jnp.stack(outs["vs"]), jnp.stack(outs["cs"]))
```

```python
import functools

import jax
import jax.numpy as jnp
from jax import lax
from jax.experimental import pallas as pl
from jax.experimental.pallas import tpu as pltpu

F32 = jnp.float32
BF16 = jnp.bfloat16

RMS_EPS = 1e-6
HEAD_DIM = 64
LANES = 128
FF_CHUNK = 256
VMEM_LIMIT = 56 * 1024 * 1024


def _params(sem):
    return pltpu.CompilerParams(dimension_semantics=sem, vmem_limit_bytes=VMEM_LIMIT)


def _resident(shape):
    nd = len(shape)
    return pl.BlockSpec(shape, lambda *_: (0,) * nd, pipeline_mode=pl.Buffered(1))


def _rms(x, g):
    ms = jnp.mean(x * x, axis=-1, keepdims=True)
    return (x * lax.rsqrt(ms + RMS_EPS)) * g


LOG2E = 1.4426950408889634
SIGN_BIT = 0x80000000


def _softplus2(z):
    neg_abs = pltpu.bitcast(pltpu.bitcast(z, jnp.uint32) | jnp.uint32(SIGN_BIT), F32)
    return jnp.maximum(z, 0.0) + jnp.log(1.0 + jnp.exp2(neg_abs)) * LOG2E


def _ffn_kernel(x_ref, g_ref, wup_ref, wdn_ref, o_ref, *, n_chunks):
    x = x_ref[...]
    h = _rms(x, g_ref[0:1, :]).astype(BF16)
    acc = jnp.zeros(x.shape, F32)
    for j in range(n_chunks):
        gate = jnp.dot(h, wup_ref[j], preferred_element_type=F32)
        up = jnp.dot(h, wup_ref[n_chunks + j], preferred_element_type=F32)
        a = ((gate * jax.nn.sigmoid(gate)) * up).astype(BF16)
        acc = acc + jnp.dot(a, wdn_ref[j], preferred_element_type=F32)
    o_ref[...] = x + 0.5 * _rms(acc, g_ref[1:2, :])


def _ffn_half(x, g2, wup_r, wdn_r, tm):
    n, d = x.shape
    n_chunks = wdn_r.shape[0]
    return pl.pallas_call(
        functools.partial(_ffn_kernel, n_chunks=n_chunks),
        grid=(n // tm,),
        in_specs=[
            pl.BlockSpec((tm, d), lambda i: (i, 0)),
            _resident(g2.shape),
            _resident(wup_r.shape),
            _resident(wdn_r.shape),
        ],
        out_specs=pl.BlockSpec((tm, d), lambda i: (i, 0)),
        out_shape=jax.ShapeDtypeStruct((n, d), F32),
        compiler_params=_params(("parallel",)),
        name="ffn_half",
    )(x, g2, wup_r, wdn_r)


def _attn_pre_kernel(x_ref, g_ref, w_ref, q_ref, kf_ref, vf_ref, kb_ref, vb_ref):
    h = _rms(x_ref[...], g_ref[...]).astype(BF16)
    q = jnp.dot(h, w_ref[0], preferred_element_type=F32)
    q_ref[...] = (q * (HEAD_DIM ** -0.5 * LOG2E)).astype(BF16)
    k = jnp.dot(h, w_ref[1], preferred_element_type=F32)
    kf_ref[...] = k
    kb_ref[...] = k.astype(BF16)
    v = jnp.dot(h, w_ref[2], preferred_element_type=F32)
    vf_ref[...] = v
    vb_ref[...] = v.astype(BF16)


def _attn_pre(x, g, w3, tm):
    n, d = x.shape
    row = pl.BlockSpec((tm, d), lambda i: (i, 0))
    return pl.pallas_call(
        _attn_pre_kernel,
        grid=(n // tm,),
        in_specs=[row, _resident(g.shape), _resident(w3.shape)],
        out_specs=[row] * 5,
        out_shape=[jax.ShapeDtypeStruct((n, d), dt) for dt in (BF16, F32, F32, BF16, BF16)],
        compiler_params=_params(("parallel",)),
        name="attn_pre",
    )(x, g, w3)


def _mix_post_kernel(o_ref, x_ref, g_ref, w_ref, y_ref):
    m = jnp.dot(o_ref[...], w_ref[...], preferred_element_type=F32)
    y_ref[...] = x_ref[...] + _rms(m, g_ref[...])


def _mix_post(o, x, g, w, tm):
    n, d = x.shape
    row = pl.BlockSpec((tm, d), lambda i: (i, 0))
    return pl.pallas_call(
        _mix_post_kernel,
        grid=(n // tm,),
        in_specs=[row, row, _resident(g.shape), _resident(w.shape)],
        out_specs=row,
        out_shape=jax.ShapeDtypeStruct((n, d), F32),
        compiler_params=_params(("parallel",)),
        name="mix_post",
    )(o, x, g, w)


def _later_matrix(tk):
    j = lax.broadcasted_iota(jnp.int32, (tk, tk), 0)
    s = lax.broadcasted_iota(jnp.int32, (tk, tk), 1)
    return jnp.where(j > s, 1.0, 0.0).astype(BF16)


def _sb_logits_stage(z, causal):
    sp = _softplus2(z)
    lsig = z - sp
    if causal is not None:
        sp = jnp.where(causal, sp, 0.0)
        lsig = jnp.where(causal, lsig, -jnp.inf)
    return lsig, sp.astype(BF16), jnp.sum(sp, axis=-1, keepdims=True)


def _sb_weights_stage(lsig, spb, c, later_mat):
    later = jnp.dot(spb, later_mat, preferred_element_type=F32)
    return jnp.exp2(lsig - later - c)


PAIRS_PER_STEP = 4


def _prompt_attn_kernel(bias_ref, q_ref, k_ref, v_ref, o_ref, lsig_ref, spb_ref, sum_ref, c_ref, acc_ref, *, tq):
    i = pl.program_id(2)
    lane = lax.broadcasted_iota(jnp.int32, (tq, LANES), 1)
    row = lax.broadcasted_iota(jnp.int32, (2 * tq, 1), 0)
    later_mat = _later_matrix(tq)
    lanes_of = lambda g: slice(g * LANES, (g + 1) * LANES)
    rows_of = lambda kt: pl.ds(pl.multiple_of(kt * tq, tq), tq)

    qs, bcol = [], []
    for g in range(PAIRS_PER_STEP):
        q = q_ref[:, lanes_of(g)]
        zero = jnp.zeros_like(q)
        qs.append(jnp.concatenate([jnp.where(lane < HEAD_DIM, q, zero), jnp.where(lane >= HEAD_DIM, q, zero)], axis=0))
        head = 2 * (PAIRS_PER_STEP * pl.program_id(1) + g)
        bcol.append(jnp.where(row < tq, bias_ref[head], bias_ref[head + 1]))

    def start_tile(g, kt, causal):
        z = lax.dot_general(qs[g], k_ref[rows_of(kt), lanes_of(g)], (((1,), (1,)), ((), ())),
                            preferred_element_type=F32) + bcol[g]
        lsig_ref[g], spb_ref[g], sum_ref[g] = _sb_logits_stage(z, causal)

    def finish_tile(g, kt):
        w = _sb_weights_stage(lsig_ref[g], spb_ref[g], c_ref[g], later_mat)
        c_ref[g] += sum_ref[g]
        acc_ref[g] += jnp.dot(w.astype(BF16), v_ref[rows_of(kt), lanes_of(g)], preferred_element_type=F32)

    t = lax.broadcasted_iota(jnp.int32, (2 * tq, tq), 0)
    t = jnp.where(t >= tq, t - tq, t)
    s = lax.broadcasted_iota(jnp.int32, (2 * tq, tq), 1)
    c_ref[...] = jnp.zeros_like(c_ref)
    acc_ref[...] = jnp.zeros_like(acc_ref)
    for g in range(PAIRS_PER_STEP):
        start_tile(g, i, s < t)

    @pl.loop(0, i)
    def _(m):
        for g in range(PAIRS_PER_STEP):
            finish_tile(g, i - m)
            start_tile(g, i - 1 - m, None)

    for g in range(PAIRS_PER_STEP):
        finish_tile(g, 0)
        acc = acc_ref[g]
        o_ref[:, lanes_of(g)] = jnp.where(lane < HEAD_DIM, acc[:tq], acc[tq:]).astype(o_ref.dtype)


def _prompt_attention(q, k, v, bias, batch, seq, tq):
    n, d = q.shape
    width = PAIRS_PER_STEP * LANES
    assert d % width == 0
    nq = seq // tq
    q_spec = pl.BlockSpec((tq, width), lambda b, p, i: (b * nq + i, p))
    kv_spec = pl.BlockSpec((seq, width), lambda b, p, i: (b, p))
    per_pair = lambda cols, dt: pltpu.VMEM((PAIRS_PER_STEP, 2 * tq, cols), dt)
    return pl.pallas_call(
        functools.partial(_prompt_attn_kernel, tq=tq),
        grid=(batch, d // width, nq),
        in_specs=[pl.BlockSpec(memory_space=pltpu.SMEM), q_spec, kv_spec, kv_spec],
        out_specs=q_spec,
        out_shape=jax.ShapeDtypeStruct((n, d), BF16),
        scratch_shapes=[per_pair(tq, F32), per_pair(tq, BF16), per_pair(1, F32), per_pair(1, F32),
                        per_pair(LANES, F32)],
        compiler_params=_params(("parallel", "parallel", "arbitrary")),
        name="prompt_attn",
    )(bias, q, k, v)


PAGES_PER_TILE = 2


def _sample_attn_kernel(pt_ref, qbd_ref, bcol_ref, kn_ref, vn_ref, *rest, n_heads, t_new, pps):
    del pt_ref
    k_refs = rest[:pps]
    v_refs = rest[pps:2 * pps]
    o_ref, acc_ref, c_ref = rest[2 * pps:]
    j = pl.program_id(1)
    rows = n_heads * t_new
    page = kn_ref.shape[1]
    qbd = qbd_ref[...]
    bcol = bcol_ref[...]

    def tile(kt, vt, causal):
        z = jnp.dot(qbd, kt, preferred_element_type=F32) + bcol
        lsig, spb, row_sum = _sb_logits_stage(z, causal)
        w = _sb_weights_stage(lsig, spb, c_ref[...], _later_matrix(kt.shape[1]))
        c_ref[...] += row_sum
        acc_ref[...] += lax.dot_general(w.astype(BF16), vt, (((1,), (1,)), ((), ())), preferred_element_type=F32)

    @pl.when(j == 0)
    def _():
        acc_ref[...] = jnp.zeros_like(acc_ref)
        c_ref[...] = jnp.zeros_like(c_ref)
        t = lax.broadcasted_iota(jnp.int32, (rows, page), 0) % t_new
        s = lax.broadcasted_iota(jnp.int32, (rows, page), 1)
        tile(kn_ref[...], vn_ref[...], s < t)

    for g in range(0, pps, PAGES_PER_TILE):
        group = range(g + PAGES_PER_TILE - 1, g - 1, -1)
        kt = jnp.concatenate([k_refs[r][...].astype(BF16) for r in group], axis=1)
        vt = jnp.concatenate([v_refs[r][...].astype(BF16) for r in group], axis=1)
        tile(kt, vt, None)

    @pl.when(j == pl.num_programs(1) - 1)
    def _():
        d = acc_ref.shape[1]
        head_of_lane = lax.broadcasted_iota(jnp.int32, (t_new, d), 1) // HEAD_DIM
        o = jnp.zeros((t_new, d), F32)
        for h in range(n_heads):
            o = jnp.where(head_of_lane == h, acc_ref[h * t_new:(h + 1) * t_new, :], o)
        o_ref[...] = o.astype(o_ref.dtype)


def _sample_attention(q, k_new, v_new, bias, kt_pool, vt_pool, page_table, layer, dec_batch, t_new, pps):
    n, d = q.shape
    n_heads = d // HEAD_DIM
    rows = n_heads * t_new
    page = kt_pool.shape[3]
    n_pages = page_table.shape[1]
    assert n_pages % pps == 0 and pps % PAGES_PER_TILE == 0 and t_new <= page
    q4 = q.reshape(dec_batch, t_new, n_heads, HEAD_DIM)
    eye = jnp.eye(n_heads, dtype=q.dtype)
    qbd = jnp.einsum("nthd,hg->nhtgd", q4, eye).reshape(dec_batch, rows, d)
    bcol = jnp.repeat(bias.astype(F32), t_new).reshape(rows, 1)
    pad = ((0, 0), (0, 0), (0, page - t_new))
    kn = jnp.pad(k_new.reshape(dec_batch, t_new, d).transpose(0, 2, 1), pad)
    vn = jnp.pad(v_new.reshape(dec_batch, t_new, d).transpose(0, 2, 1), pad)

    per_seq = lambda shape: pl.BlockSpec((None,) + shape, lambda b, j, pt: (b, 0, 0))

    def page_spec(r):
        return pl.BlockSpec((None, None, d, page),
                            lambda b, j, pt: (layer, pt[b, n_pages - 1 - (j * pps + r)], 0, 0))

    grid_spec = pltpu.PrefetchScalarGridSpec(
        num_scalar_prefetch=1,
        grid=(dec_batch, n_pages // pps),
        in_specs=[per_seq((rows, d)), pl.BlockSpec((rows, 1), lambda b, j, pt: (0, 0)),
                  per_seq((d, page)), per_seq((d, page))]
        + [page_spec(r) for r in range(pps)] * 2,
        out_specs=per_seq((t_new, d)),
        scratch_shapes=[pltpu.VMEM((rows, d), F32), pltpu.VMEM((rows, 1), F32)],
    )
    o = pl.pallas_call(
        functools.partial(_sample_attn_kernel, n_heads=n_heads, t_new=t_new, pps=pps),
        grid_spec=grid_spec,
        out_shape=jax.ShapeDtypeStruct((dec_batch, t_new, d), BF16),
        compiler_params=_params(("parallel", "arbitrary")),
        name="sample_attn",
    )(page_table, qbd, bcol, kn, vn, *([kt_pool] * pps), *([vt_pool] * pps))
    return o.reshape(n, d)


def _conv_pre_kernel(x_ref, g_ref, w_ref, u_ref, b_ref):
    h = _rms(x_ref[...], g_ref[...]).astype(BF16)
    b_ref[...] = jnp.dot(h, w_ref[0], preferred_element_type=F32)
    c_gate = jnp.dot(h, w_ref[1], preferred_element_type=F32)
    val = jnp.dot(h, w_ref[2], preferred_element_type=F32)
    u_ref[...] = c_gate * val


def _conv_pre(x, g, w3, tm):
    n, d = x.shape
    row = pl.BlockSpec((tm, d), lambda i: (i, 0))
    return pl.pallas_call(
        _conv_pre_kernel,
        grid=(n // tm,),
        in_specs=[row, _resident(g.shape), _resident(w3.shape)],
        out_specs=[row, row],
        out_shape=[jax.ShapeDtypeStruct((n, d), F32)] * 2,
        compiler_params=_params(("parallel",)),
        name="conv_pre",
    )(x, g, w3)


def _conv_post_kernel(u_ref, halo_ref, b_ref, x_ref, cw_ref, g_ref, w_ref, y_ref, *, tiles_per_seq, zero_state):
    u = u_ref[...]
    prev1 = halo_ref[7:8, :]
    prev2 = halo_ref[6:7, :]
    if zero_state:
        first = (pl.program_id(0) % tiles_per_seq) == 0
        prev1 = jnp.where(first, 0.0, prev1)
        prev2 = jnp.where(first, 0.0, prev2)
    row = lax.broadcasted_iota(jnp.int32, u.shape, 0)
    u1 = jnp.where(row == 0, prev1, pltpu.roll(u, 1, 0))
    u2 = jnp.where(row == 0, prev2, jnp.where(row == 1, prev1, pltpu.roll(u, 2, 0)))
    conv = cw_ref[0:1, :] * u2 + cw_ref[1:2, :] * u1 + cw_ref[2:3, :] * u
    y = (b_ref[...] * conv).astype(BF16)
    m = jnp.dot(y, w_ref[...], preferred_element_type=F32)
    y_ref[...] = x_ref[...] + _rms(m, g_ref[...])


def _conv_post(u, halo_src, b, x, conv_w, g, w, tm, seq, zero_state):
    n, d = x.shape
    assert seq % tm == 0 and tm % 8 == 0
    row = pl.BlockSpec((tm, d), lambda i: (i, 0))
    if zero_state:
        halo = pl.BlockSpec((8, d), lambda i: (jnp.maximum(i * (tm // 8) - 1, 0), 0))
    else:
        halo = pl.BlockSpec((8, d), lambda i: (i, 0))
    return pl.pallas_call(
        functools.partial(_conv_post_kernel, tiles_per_seq=seq // tm, zero_state=zero_state),
        grid=(n // tm,),
        in_specs=[row, halo, row, row, _resident(conv_w.shape), _resident(g.shape), _resident(w.shape)],
        out_specs=row,
        out_shape=jax.ShapeDtypeStruct((n, d), F32),
        compiler_params=_params(("parallel",)),
        name="conv_post",
    )(u, halo_src, b, x, conv_w, g, w)


def _row_tile(n, target):
    tm = min(n, target)
    assert n % tm == 0
    return tm


def kernel(x_prompt, x_sample, cache_k, cache_v, state_conv, page_table, norm_gain,
           w_ffn_up, w_ffn_down, w_attn_qkv, w_attn_out, attn_logit_bias, w_conv_in, conv_w, w_conv_out):
    batch, seq, d = x_prompt.shape
    dec_batch, t_new, _ = x_sample.shape
    depth = norm_gain.shape[0]
    d_ff = w_ffn_down.shape[2]
    n_heads = attn_logit_bias.shape[1]
    assert d == n_heads * HEAD_DIM and d % LANES == 0 and d_ff % FF_CHUNK == 0
    n_chunks = d_ff // FF_CHUNK
    n_mixers = 2

    xp = x_prompt.reshape(batch * seq, d)
    xs = x_sample.reshape(dec_batch * t_new, d)
    tm_p = _row_tile(batch * seq, 512)
    tm_s = _row_tile(dec_batch * t_new, 256)
    tq = _row_tile(seq, 256)
    assert seq % tm_p == 0

    wup_r = w_ffn_up.astype(BF16).reshape(depth, 2, d, 2 * n_chunks, FF_CHUNK).transpose(0, 1, 3, 2, 4)
    wdn_r = w_ffn_down.astype(BF16).reshape(depth, 2, n_chunks, FF_CHUNK, d)
    split3 = lambda w: w.astype(BF16).reshape(w.shape[0], d, 3, d).transpose(0, 2, 1, 3)
    wqkv_r = split3(w_attn_qkv)
    wcin_r = split3(w_conv_in)
    wao = w_attn_out.astype(BF16)
    wco = w_conv_out.astype(BF16)
    to_feature_major = lambda pool: pool.transpose(0, 1, 3, 4, 2).reshape(pool.shape[:2] + (d, pool.shape[2]))
    pool_kt = to_feature_major(cache_k)
    pool_vt = to_feature_major(cache_v)

    outs = {name: [] for name in ("kp", "vp", "cp", "ks", "vs", "cs")}
    for i in range(depth):
        g = norm_gain[i]
        ffn = lambda x, half, tm: _ffn_half(x, g[4 * half:4 * half + 2], wup_r[i, half], wdn_r[i, half], tm)
        xp = ffn(xp, 0, tm_p)
        xs = ffn(xs, 0, tm_s)
        g_pre, g_post = g[2:3], g[3:4]
        if i % n_mixers == 0:
            a = i // n_mixers
            bias = attn_logit_bias[a].astype(F32) * LOG2E
            qp, kpf, vpf, kpb, vpb = _attn_pre(xp, g_pre, wqkv_r[a], tm_p)
            qs, ksf, vsf, ksb, vsb = _attn_pre(xs, g_pre, wqkv_r[a], tm_s)
            op = _prompt_attention(qp, kpb, vpb, bias, batch, seq, tq)
            os_ = _sample_attention(qs, ksb, vsb, bias, pool_kt, pool_vt, page_table, a, dec_batch, t_new, pps=4)
            xp = _mix_post(op, xp, g_post, wao[a], tm_p)
            xs = _mix_post(os_, xs, g_post, wao[a], tm_s)
            kv_p = (batch, seq, n_heads, HEAD_DIM)
            kv_s = (dec_batch, t_new, n_heads, HEAD_DIM)
            outs["kp"].append(kpf.reshape(kv_p)); outs["vp"].append(vpf.reshape(kv_p))
            outs["ks"].append(ksf.reshape(kv_s)); outs["vs"].append(vsf.reshape(kv_s))
        else:
            c = i // n_mixers
            n_state = state_conv.shape[2]
            assert n_state == conv_w.shape[1] - 1 == 2 and seq >= n_state and t_new >= n_state and t_new == 8
            up_, bp = _conv_pre(xp, g_pre, wcin_r[c], tm_p)
            us_, bs = _conv_pre(xs, g_pre, wcin_r[c], tm_s)
            xp = _conv_post(up_, up_, bp, xp, conv_w[c], g_post, wco[c], tm_p, seq, zero_state=True)
            halo_s = jnp.pad(state_conv[c], ((0, 0), (8 - n_state, 0), (0, 0))).reshape(dec_batch * 8, d)
            xs = _conv_post(us_, halo_s, bs, xs, conv_w[c], g_post, wco[c], t_new, t_new, zero_state=False)
            outs["cp"].append(up_.reshape(batch, seq, d)[:, seq - n_state:])
            outs["cs"].append(us_.reshape(dec_batch, t_new, d)[:, t_new - n_state:])
        xp = ffn(xp, 1, tm_p)
        xs = ffn(xs, 1, tm_s)

    return (xp.reshape(batch, seq, d), xs.reshape(dec_batch, t_new, d),
            jnp.stack(outs["kp"]), jnp.stack(outs["vp"]), jnp.stack(outs["cp"]),
            jnp.stack(outs["ks"]), jnp.stack(outs["vs"]), jnp.stack(outs["cs"]))
```

```python
import functools

import jax
import jax.numpy as jnp
from jax import lax
from jax.experimental import pallas as pl
from jax.experimental.pallas import tpu as pltpu

F32 = jnp.float32
BF16 = jnp.bfloat16

RMS_EPS = 1e-6
HEAD_DIM = 64
LANES = 128
FF_CHUNK = 256
VMEM_LIMIT = 56 * 1024 * 1024


def _params(sem):
    return pltpu.CompilerParams(dimension_semantics=sem, vmem_limit_bytes=VMEM_LIMIT)


def _resident(shape):
    nd = len(shape)
    return pl.BlockSpec(shape, lambda *_: (0,) * nd, pipeline_mode=pl.Buffered(1))


def _rms(x, g):
    ms = jnp.mean(x * x, axis=-1, keepdims=True)
    return (x * lax.rsqrt(ms + RMS_EPS)) * g


LOG2E = 1.4426950408889634
SIGN_BIT = 0x80000000


def _softplus2(z):
    neg_abs = pltpu.bitcast(pltpu.bitcast(z, jnp.uint32) | jnp.uint32(SIGN_BIT), F32)
    return jnp.maximum(z, 0.0) + jnp.log(1.0 + jnp.exp2(neg_abs)) * LOG2E


def _ffn_kernel(x_ref, g_ref, wup_ref, wdn_ref, o_ref, *, n_chunks):
    x = x_ref[...]
    h = _rms(x, g_ref[0:1, :]).astype(BF16)
    acc = jnp.zeros(x.shape, F32)
    for j in range(n_chunks):
        gate = jnp.dot(h, wup_ref[j], preferred_element_type=F32)
        up = jnp.dot(h, wup_ref[n_chunks + j], preferred_element_type=F32)
        a = ((gate * jax.nn.sigmoid(gate)) * up).astype(BF16)
        acc = acc + jnp.dot(a, wdn_ref[j], preferred_element_type=F32)
    o_ref[...] = x + 0.5 * _rms(acc, g_ref[1:2, :])


def _ffn_half(x, g2, wup_r, wdn_r, tm):
    n, d = x.shape
    n_chunks = wdn_r.shape[0]
    return pl.pallas_call(
        functools.partial(_ffn_kernel, n_chunks=n_chunks),
        grid=(n // tm,),
        in_specs=[
            pl.BlockSpec((tm, d), lambda i: (i, 0)),
            _resident(g2.shape),
            _resident(wup_r.shape),
            _resident(wdn_r.shape),
        ],
        out_specs=pl.BlockSpec((tm, d), lambda i: (i, 0)),
        out_shape=jax.ShapeDtypeStruct((n, d), F32),
        compiler_params=_params(("parallel",)),
        name="ffn_half",
    )(x, g2, wup_r, wdn_r)


def _attn_pre_kernel(x_ref, g_ref, w_ref, q_ref, ktf_ref, vtf_ref, ktb_ref, vb_ref):
    h = _rms(x_ref[...], g_ref[...]).astype(BF16)
    q = jnp.dot(h, w_ref[0], preferred_element_type=F32)
    q_ref[...] = (q * (HEAD_DIM ** -0.5 * LOG2E)).astype(BF16)
    kt = lax.dot_general(w_ref[1], h, (((1,), (1,)), ((), ())), preferred_element_type=F32)
    ktf_ref[...] = kt
    ktb_ref[...] = kt.astype(BF16)
    v = jnp.dot(h, w_ref[2], preferred_element_type=F32)
    vtf_ref[...] = v.T
    vb_ref[...] = v.astype(BF16)


def _attn_pre(x, g, w3, tm, batch, seq):
    n, d = x.shape
    tiles_per_seq = seq // tm
    assert n == batch * seq and seq % tm == 0
    row = pl.BlockSpec((tm, d), lambda i: (i, 0))
    col = pl.BlockSpec((None, d, tm), lambda i: (i // tiles_per_seq, 0, i % tiles_per_seq))
    feature_major = lambda dt: jax.ShapeDtypeStruct((batch, d, seq), dt)
    return pl.pallas_call(
        _attn_pre_kernel,
        grid=(n // tm,),
        in_specs=[row, _resident(g.shape), _resident(w3.shape)],
        out_specs=[row, col, col, col, row],
        out_shape=[jax.ShapeDtypeStruct((n, d), BF16), feature_major(F32), feature_major(F32),
                   feature_major(BF16), jax.ShapeDtypeStruct((n, d), BF16)],
        compiler_params=_params(("parallel",)),
        name="attn_pre",
    )(x, g, w3)


def _mix_post_kernel(o_ref, x_ref, g_ref, w_ref, y_ref):
    m = jnp.dot(o_ref[...], w_ref[...], preferred_element_type=F32)
    y_ref[...] = x_ref[...] + _rms(m, g_ref[...])


def _mix_post(o, x, g, w, tm):
    n, d = x.shape
    row = pl.BlockSpec((tm, d), lambda i: (i, 0))
    return pl.pallas_call(
        _mix_post_kernel,
        grid=(n // tm,),
        in_specs=[row, row, _resident(g.shape), _resident(w.shape)],
        out_specs=row,
        out_shape=jax.ShapeDtypeStruct((n, d), F32),
        compiler_params=_params(("parallel",)),
        name="mix_post",
    )(o, x, g, w)


def _later_matrix(tk):
    j = lax.broadcasted_iota(jnp.int32, (tk, tk + LANES), 0)
    s = lax.broadcasted_iota(jnp.int32, (tk, tk + LANES), 1)
    return jnp.where((j > s) | (s >= tk), 1.0, 0.0).astype(BF16)


def _sb_logits_stage(z, causal):
    sp = _softplus2(z)
    lsig = z - sp
    if causal is not None:
        sp = jnp.where(causal, sp, 0.0)
        lsig = jnp.where(causal, lsig, -jnp.inf)
    return lsig, sp.astype(BF16)


def _sb_later_stage(spb, later_mat):
    tk = spb.shape[1]
    sums = jnp.dot(spb, later_mat, preferred_element_type=F32)
    return sums[:, :tk], sums[:, tk:]


def _sb_weights_stage(lsig, later, c):
    return jnp.exp2(lsig - later - jnp.concatenate([c] * (lsig.shape[1] // LANES), axis=1))


PAIRS_PER_STEP = 4


def _prompt_attn_kernel(bias_ref, q_ref, kt_ref, v_ref, o_ref,
                        z_ref, lsig_ref, sum_ref, spb_ref, later_ref, c_ref, acc_ref, *, tq):
    i = pl.program_id(2)
    lane = lax.broadcasted_iota(jnp.int32, (tq, LANES), 1)
    row = lax.broadcasted_iota(jnp.int32, (2 * tq, 1), 0)
    later_mat = _later_matrix(tq)
    lanes_of = lambda g: slice(g * LANES, (g + 1) * LANES)
    rows_of = lambda kt: pl.ds(pl.multiple_of(kt * tq, tq), tq)

    n_heads = bias_ref.shape[0] // 2
    lane2 = lax.broadcasted_iota(jnp.int32, (2 * tq, LANES), 1)
    ones_rows = jnp.where(lax.broadcasted_iota(jnp.int32, (LANES, tq), 0) < 2, 1.0, 0.0).astype(BF16)
    qs = []
    for g in range(PAIRS_PER_STEP):
        q = q_ref[:, lanes_of(g)]
        zero = jnp.zeros_like(q)
        q2 = jnp.concatenate([jnp.where(lane < HEAD_DIM, q, zero), jnp.where(lane >= HEAD_DIM, q, zero)], axis=0)
        head = 2 * (PAIRS_PER_STEP * pl.program_id(1) + g)
        hi = jnp.where(row < tq, bias_ref[head], bias_ref[head + 1])
        lo = jnp.where(row < tq, bias_ref[n_heads + head], bias_ref[n_heads + head + 1])
        bias_cols = jnp.where(lane2 == 0, hi, jnp.where(lane2 == 1, lo, 0.0)).astype(BF16)
        qs.append(jnp.concatenate([q2, bias_cols], axis=1))

    groups = range(PAIRS_PER_STEP)

    def qk(m):
        cols = rows_of(jnp.maximum(i - m, 0))
        for g in groups:
            keys = jnp.concatenate([kt_ref[lanes_of(g), cols], ones_rows], axis=0)
            z_ref[g] = jnp.dot(qs[g], keys, preferred_element_type=F32)

    def softplus(m, causal):
        for g in groups:
            lsig_ref[m % 2, g], spb_ref[g] = _sb_logits_stage(z_ref[g], causal)

    def later():
        for g in groups:
            later_ref[g], sum_ref[g] = _sb_later_stage(spb_ref[g], later_mat)

    def weights(m):
        for g in groups:
            c = c_ref[g]
            w = _sb_weights_stage(lsig_ref[m % 2, g], later_ref[g], c)
            c_ref[g] = c + sum_ref[g]
            acc_ref[g] += jnp.dot(w.astype(BF16), v_ref[rows_of(i - m), lanes_of(g)], preferred_element_type=F32)

    t = lax.broadcasted_iota(jnp.int32, (2 * tq, tq), 0)
    t = jnp.where(t >= tq, t - tq, t)
    s = lax.broadcasted_iota(jnp.int32, (2 * tq, tq), 1)
    c_ref[...] = jnp.zeros_like(c_ref)
    acc_ref[...] = jnp.zeros_like(acc_ref)
    qk(0)
    softplus(0, s < t)
    qk(1)
    later()

    @pl.when(i >= 1)
    def _():
        softplus(1, None)
        qk(2)

    @pl.loop(1, i)
    def _(m):
        weights(m - 1)
        later()
        softplus(m + 1, None)
        qk(m + 2)

    @pl.when(i >= 1)
    def _():
        weights(i - 1)
        later()

    weights(i)
    for g in groups:
        acc = acc_ref[g]
        o_ref[:, lanes_of(g)] = jnp.where(lane < HEAD_DIM, acc[:tq], acc[tq:]).astype(o_ref.dtype)


def _prompt_attention(q, kt, v, bias, batch, seq, tq):
    n, d = q.shape
    width = PAIRS_PER_STEP * LANES
    assert d % width == 0
    nq = seq // tq
    q_spec = pl.BlockSpec((tq, width), lambda b, p, i: (b * nq + i, p))
    kt_spec = pl.BlockSpec((None, width, seq), lambda b, p, i: (b, p, 0))
    v_spec = pl.BlockSpec((seq, width), lambda b, p, i: (b, p))
    hi = bias.astype(BF16).astype(F32)
    bias = jnp.concatenate([hi, (bias - hi).astype(BF16).astype(F32)])
    per_pair = lambda cols, dt: pltpu.VMEM((PAIRS_PER_STEP, 2 * tq, cols), dt)
    ring = lambda cols, dt: pltpu.VMEM((2, PAIRS_PER_STEP, 2 * tq, cols), dt)
    return pl.pallas_call(
        functools.partial(_prompt_attn_kernel, tq=tq),
        grid=(batch, d // width, nq),
        in_specs=[pl.BlockSpec(memory_space=pltpu.SMEM), q_spec, kt_spec, v_spec],
        out_specs=q_spec,
        out_shape=jax.ShapeDtypeStruct((n, d), BF16),
        scratch_shapes=[per_pair(tq, F32), ring(tq, F32), per_pair(LANES, F32), per_pair(tq, BF16),
                        per_pair(tq, F32), per_pair(LANES, F32), per_pair(LANES, F32)],
        compiler_params=_params(("parallel", "parallel", "arbitrary")),
        name="prompt_attn",
    )(bias, q, kt, v)


PAGES_PER_TILE = 2


def _sample_attn_kernel(pt_ref, qbd_ref, bcol_ref, kn_ref, vn_ref, *rest, n_heads, t_new, pps):
    del pt_ref
    k_refs = rest[:pps]
    v_refs = rest[pps:2 * pps]
    o_ref, acc_ref, c_ref = rest[2 * pps:]
    j = pl.program_id(1)
    rows = n_heads * t_new
    page = kn_ref.shape[1]
    qbd = qbd_ref[...]
    bcol = bcol_ref[...]

    def tile(kt, vt, causal):
        z = jnp.dot(qbd, kt, preferred_element_type=F32) + bcol
        lsig, spb = _sb_logits_stage(z, causal)
        later, row_sum = _sb_later_stage(spb, _later_matrix(kt.shape[1]))
        c = c_ref[...]
        w = _sb_weights_stage(lsig, later, c)
        c_ref[...] = c + row_sum
        acc_ref[...] += lax.dot_general(w.astype(BF16), vt, (((1,), (1,)), ((), ())), preferred_element_type=F32)

    @pl.when(j == 0)
    def _():
        acc_ref[...] = jnp.zeros_like(acc_ref)
        c_ref[...] = jnp.zeros_like(c_ref)
        t = lax.broadcasted_iota(jnp.int32, (rows, page), 0) % t_new
        s = lax.broadcasted_iota(jnp.int32, (rows, page), 1)
        tile(kn_ref[...], vn_ref[...], s < t)

    for g in range(0, pps, PAGES_PER_TILE):
        group = range(g + PAGES_PER_TILE - 1, g - 1, -1)
        kt = jnp.concatenate([k_refs[r][...].astype(BF16) for r in group], axis=1)
        vt = jnp.concatenate([v_refs[r][...].astype(BF16) for r in group], axis=1)
        tile(kt, vt, None)

    @pl.when(j == pl.num_programs(1) - 1)
    def _():
        d = acc_ref.shape[1]
        head_of_lane = lax.broadcasted_iota(jnp.int32, (t_new, d), 1) // HEAD_DIM
        o = jnp.zeros((t_new, d), F32)
        for h in range(n_heads):
            o = jnp.where(head_of_lane == h, acc_ref[h * t_new:(h + 1) * t_new, :], o)
        o_ref[...] = o.astype(o_ref.dtype)


def _sample_attention(q, kt_new, v_new, bias, kt_pool, vt_pool, page_table, layer, dec_batch, t_new, pps):
    n, d = q.shape
    n_heads = d // HEAD_DIM
    rows = n_heads * t_new
    page = kt_pool.shape[3]
    n_pages = page_table.shape[1]
    assert n_pages % pps == 0 and pps % PAGES_PER_TILE == 0 and t_new <= page
    q4 = q.reshape(dec_batch, t_new, n_heads, HEAD_DIM)
    eye = jnp.eye(n_heads, dtype=q.dtype)
    qbd = jnp.einsum("nthd,hg->nhtgd", q4, eye).reshape(dec_batch, rows, d)
    bcol = jnp.repeat(bias.astype(F32), t_new).reshape(rows, 1)
    pad = ((0, 0), (0, 0), (0, page - t_new))
    kn = jnp.pad(kt_new, pad)
    vn = jnp.pad(v_new.reshape(dec_batch, t_new, d).transpose(0, 2, 1), pad)

    per_seq = lambda shape: pl.BlockSpec((None,) + shape, lambda b, j, pt: (b, 0, 0))

    def page_spec(r):
        return pl.BlockSpec((None, None, d, page),
                            lambda b, j, pt: (layer, pt[b, n_pages - 1 - (j * pps + r)], 0, 0))

    grid_spec = pltpu.PrefetchScalarGridSpec(
        num_scalar_prefetch=1,
        grid=(dec_batch, n_pages // pps),
        in_specs=[per_seq((rows, d)), pl.BlockSpec((rows, 1), lambda b, j, pt: (0, 0)),
                  per_seq((d, page)), per_seq((d, page))]
        + [page_spec(r) for r in range(pps)] * 2,
        out_specs=per_seq((t_new, d)),
        scratch_shapes=[pltpu.VMEM((rows, d), F32), pltpu.VMEM((rows, LANES), F32)],
    )
    o = pl.pallas_call(
        functools.partial(_sample_attn_kernel, n_heads=n_heads, t_new=t_new, pps=pps),
        grid_spec=grid_spec,
        out_shape=jax.ShapeDtypeStruct((dec_batch, t_new, d), BF16),
        compiler_params=_params(("parallel", "arbitrary")),
        name="sample_attn",
    )(page_table, qbd, bcol, kn, vn, *([kt_pool] * pps), *([vt_pool] * pps))
    return o.reshape(n, d)


def _conv_pre_kernel(x_ref, g_ref, w_ref, u_ref, b_ref):
    h = _rms(x_ref[...], g_ref[...]).astype(BF16)
    b_ref[...] = jnp.dot(h, w_ref[0], preferred_element_type=F32)
    c_gate = jnp.dot(h, w_ref[1], preferred_element_type=F32)
    val = jnp.dot(h, w_ref[2], preferred_element_type=F32)
    u_ref[...] = c_gate * val


def _conv_pre(x, g, w3, tm):
    n, d = x.shape
    row = pl.BlockSpec((tm, d), lambda i: (i, 0))
    return pl.pallas_call(
        _conv_pre_kernel,
        grid=(n // tm,),
        in_specs=[row, _resident(g.shape), _resident(w3.shape)],
        out_specs=[row, row],
        out_shape=[jax.ShapeDtypeStruct((n, d), F32)] * 2,
        compiler_params=_params(("parallel",)),
        name="conv_pre",
    )(x, g, w3)


def _conv_post_kernel(u_ref, halo_ref, b_ref, x_ref, cw_ref, g_ref, w_ref, y_ref, *, tiles_per_seq, zero_state):
    u = u_ref[...]
    prev1 = halo_ref[7:8, :]
    prev2 = halo_ref[6:7, :]
    if zero_state:
        first = (pl.program_id(0) % tiles_per_seq) == 0
        prev1 = jnp.where(first, 0.0, prev1)
        prev2 = jnp.where(first, 0.0, prev2)
    row = lax.broadcasted_iota(jnp.int32, u.shape, 0)
    u1 = jnp.where(row == 0, prev1, pltpu.roll(u, 1, 0))
    u2 = jnp.where(row == 0, prev2, jnp.where(row == 1, prev1, pltpu.roll(u, 2, 0)))
    conv = cw_ref[0:1, :] * u2 + cw_ref[1:2, :] * u1 + cw_ref[2:3, :] * u
    y = (b_ref[...] * conv).astype(BF16)
    m = jnp.dot(y, w_ref[...], preferred_element_type=F32)
    y_ref[...] = x_ref[...] + _rms(m, g_ref[...])


def _conv_post(u, halo_src, b, x, conv_w, g, w, tm, seq, zero_state):
    n, d = x.shape
    assert seq % tm == 0 and tm % 8 == 0
    row = pl.BlockSpec((tm, d), lambda i: (i, 0))
    if zero_state:
        halo = pl.BlockSpec((8, d), lambda i: (jnp.maximum(i * (tm // 8) - 1, 0), 0))
    else:
        halo = pl.BlockSpec((8, d), lambda i: (i, 0))
    return pl.pallas_call(
        functools.partial(_conv_post_kernel, tiles_per_seq=seq // tm, zero_state=zero_state),
        grid=(n // tm,),
        in_specs=[row, halo, row, row, _resident(conv_w.shape), _resident(g.shape), _resident(w.shape)],
        out_specs=row,
        out_shape=jax.ShapeDtypeStruct((n, d), F32),
        compiler_params=_params(("parallel",)),
        name="conv_post",
    )(u, halo_src, b, x, conv_w, g, w)


def _row_tile(n, target):
    tm = min(n, target)
    assert n % tm == 0
    return tm


def kernel(x_prompt, x_sample, cache_k, cache_v, state_conv, page_table, norm_gain,
           w_ffn_up, w_ffn_down, w_attn_qkv, w_attn_out, attn_logit_bias, w_conv_in, conv_w, w_conv_out):
    batch, seq, d = x_prompt.shape
    dec_batch, t_new, _ = x_sample.shape
    depth = norm_gain.shape[0]
    d_ff = w_ffn_down.shape[2]
    n_heads = attn_logit_bias.shape[1]
    assert d == n_heads * HEAD_DIM and d % LANES == 0 and d_ff % FF_CHUNK == 0
    n_chunks = d_ff // FF_CHUNK
    n_mixers = 2

    xp = x_prompt.reshape(batch * seq, d)
    xs = x_sample.reshape(dec_batch * t_new, d)
    tm_p = _row_tile(batch * seq, 512)
    tm_s = _row_tile(dec_batch * t_new, 256)
    tq = _row_tile(seq, 256)
    assert seq % tm_p == 0

    wup_r = w_ffn_up.astype(BF16).reshape(depth, 2, d, 2 * n_chunks, FF_CHUNK).transpose(0, 1, 3, 2, 4)
    wdn_r = w_ffn_down.astype(BF16).reshape(depth, 2, n_chunks, FF_CHUNK, d)
    split3 = lambda w: w.astype(BF16).reshape(w.shape[0], d, 3, d).transpose(0, 2, 1, 3)
    wqkv_r = split3(w_attn_qkv)
    wqkv_r = wqkv_r.at[:, 1].set(wqkv_r[:, 1].transpose(0, 2, 1))
    wcin_r = split3(w_conv_in)
    wao = w_attn_out.astype(BF16)
    wco = w_conv_out.astype(BF16)
    to_feature_major = lambda pool: pool.transpose(0, 1, 3, 4, 2).reshape(pool.shape[:2] + (d, pool.shape[2]))
    pool_kt = to_feature_major(cache_k)
    pool_vt = to_feature_major(cache_v)

    outs = {name: [] for name in ("kp", "vp", "cp", "ks", "vs", "cs")}
    for i in range(depth):
        g = norm_gain[i]
        ffn = lambda x, half, tm: _ffn_half(x, g[4 * half:4 * half + 2], wup_r[i, half], wdn_r[i, half], tm)
        xp = ffn(xp, 0, tm_p)
        xs = ffn(xs, 0, tm_s)
        g_pre, g_post = g[2:3], g[3:4]
        if i % n_mixers == 0:
            a = i // n_mixers
            bias = attn_logit_bias[a].astype(F32) * LOG2E
            n_s = dec_batch * t_new
            qp, kpf, vpf, kpb, vpb = _attn_pre(xp, g_pre, wqkv_r[a], tm_p, batch, seq)
            qs, ksf, vsf, ksb, vsb = _attn_pre(xs, g_pre, wqkv_r[a], tm_s, 1, n_s)
            op = _prompt_attention(qp, kpb, vpb, bias, batch, seq, tq)
            per_seq_t = lambda xt: xt.reshape(d, dec_batch, t_new).transpose(1, 0, 2)
            os_ = _sample_attention(qs, per_seq_t(ksb), vsb, bias, pool_kt, pool_vt, page_table, a,
                                    dec_batch, t_new, pps=8)
            xp = _mix_post(op, xp, g_post, wao[a], tm_p)
            xs = _mix_post(os_, xs, g_post, wao[a], tm_s)
            heads_last = lambda xt, b, t: xt.reshape(b, n_heads, HEAD_DIM, t).transpose(0, 3, 1, 2)
            outs["kp"].append(heads_last(kpf, batch, seq)); outs["vp"].append(heads_last(vpf, batch, seq))
            outs["ks"].append(heads_last(per_seq_t(ksf), dec_batch, t_new))
            outs["vs"].append(heads_last(per_seq_t(vsf), dec_batch, t_new))
        else:
            c = i // n_mixers
            n_state = state_conv.shape[2]
            assert n_state == conv_w.shape[1] - 1 == 2 and seq >= n_state and t_new >= n_state and t_new == 8
            up_, bp = _conv_pre(xp, g_pre, wcin_r[c], tm_p)
            us_, bs = _conv_pre(xs, g_pre, wcin_r[c], tm_s)
            xp = _conv_post(up_, up_, bp, xp, conv_w[c], g_post, wco[c], tm_p, seq, zero_state=True)
            halo_s = jnp.pad(state_conv[c], ((0, 0), (8 - n_state, 0), (0, 0))).reshape(dec_batch * 8, d)
            xs = _conv_post(us_, halo_s, bs, xs, conv_w[c], g_post, wco[c], t_new, t_new, zero_state=False)
            outs["cp"].append(up_.reshape(batch, seq, d)[:, seq - n_state:])
            outs["cs"].append(us_.reshape(dec_batch, t_new, d)[:, t_new - n_state:])
        xp = ffn(xp, 1, tm_p)
        xs = ffn(xs, 1, tm_s)

    return (xp.reshape(batch, seq, d), xs.reshape(dec_batch, t_new, d),
            jnp.stack(outs["kp"]), jnp.stack(outs["vp"]), jnp.stack(outs["cp"]),
            jnp.stack(outs["ks"]), jnp.stack(outs["vs"]), jnp.stack(outs["cs"]))
```

```python
import functools

import jax
import jax.numpy as jnp
from jax import lax
from jax.experimental import pallas as pl
from jax.experimental.pallas import tpu as pltpu

F32 = jnp.float32
BF16 = jnp.bfloat16

RMS_EPS = 1e-6
HEAD_DIM = 64
LANES = 128
FF_CHUNK = 256
VMEM_LIMIT = 56 * 1024 * 1024


def _params(sem):
    return pltpu.CompilerParams(dimension_semantics=sem, vmem_limit_bytes=VMEM_LIMIT)


def _resident(shape):
    nd = len(shape)
    return pl.BlockSpec(shape, lambda *_: (0,) * nd, pipeline_mode=pl.Buffered(1))


def _rms(x, g):
    ms = jnp.mean(x * x, axis=-1, keepdims=True)
    return (x * lax.rsqrt(ms + RMS_EPS)) * g


LOG2E = 1.4426950408889634
EXP2_MAX = 126.0


def _softplus2(z):
    return jnp.maximum(z, jnp.log(1.0 + jnp.exp2(jnp.minimum(z, EXP2_MAX))) * LOG2E)


def _ffn_kernel(x_ref, g_ref, wup_ref, wdn_ref, o_ref, *, n_chunks):
    x = x_ref[...]
    h = _rms(x, g_ref[0:1, :]).astype(BF16)
    acc = jnp.zeros(x.shape, F32)
    for j in range(n_chunks):
        gate = jnp.dot(h, wup_ref[j], preferred_element_type=F32)
        up = jnp.dot(h, wup_ref[n_chunks + j], preferred_element_type=F32)
        a = ((gate * jax.nn.sigmoid(gate)) * up).astype(BF16)
        acc = acc + jnp.dot(a, wdn_ref[j], preferred_element_type=F32)
    o_ref[...] = x + 0.5 * _rms(acc, g_ref[1:2, :])


def _ffn_half(x, g2, wup_r, wdn_r, tm):
    n, d = x.shape
    n_chunks = wdn_r.shape[0]
    return pl.pallas_call(
        functools.partial(_ffn_kernel, n_chunks=n_chunks),
        grid=(n // tm,),
        in_specs=[
            pl.BlockSpec((tm, d), lambda i: (i, 0)),
            _resident(g2.shape),
            _resident(wup_r.shape),
            _resident(wdn_r.shape),
        ],
        out_specs=pl.BlockSpec((tm, d), lambda i: (i, 0)),
        out_shape=jax.ShapeDtypeStruct((n, d), F32),
        compiler_params=_params(("parallel",)),
        name="ffn_half",
    )(x, g2, wup_r, wdn_r)


def _attn_pre_kernel(x_ref, g_ref, w_ref, q_ref, ktf_ref, vtf_ref, ktb_ref, vb_ref):
    h = _rms(x_ref[...], g_ref[...]).astype(BF16)
    q = jnp.dot(h, w_ref[0], preferred_element_type=F32)
    q_ref[...] = (q * (HEAD_DIM ** -0.5 * LOG2E)).astype(BF16)
    kt = lax.dot_general(w_ref[1], h, (((1,), (1,)), ((), ())), preferred_element_type=F32)
    ktf_ref[...] = kt
    ktb_ref[...] = kt.astype(BF16)
    v = jnp.dot(h, w_ref[2], preferred_element_type=F32)
    vtf_ref[...] = v.T
    vb_ref[...] = v.astype(BF16)


def _attn_pre(x, g, w3, tm, batch, seq):
    n, d = x.shape
    tiles_per_seq = seq // tm
    assert n == batch * seq and seq % tm == 0
    row = pl.BlockSpec((tm, d), lambda i: (i, 0))
    col = pl.BlockSpec((None, d, tm), lambda i: (i // tiles_per_seq, 0, i % tiles_per_seq))
    feature_major = lambda dt: jax.ShapeDtypeStruct((batch, d, seq), dt)
    return pl.pallas_call(
        _attn_pre_kernel,
        grid=(n // tm,),
        in_specs=[row, _resident(g.shape), _resident(w3.shape)],
        out_specs=[row, col, col, col, row],
        out_shape=[jax.ShapeDtypeStruct((n, d), BF16), feature_major(F32), feature_major(F32),
                   feature_major(BF16), jax.ShapeDtypeStruct((n, d), BF16)],
        compiler_params=_params(("parallel",)),
        name="attn_pre",
    )(x, g, w3)


def _mix_post_kernel(o_ref, x_ref, g_ref, w_ref, y_ref):
    m = jnp.dot(o_ref[...], w_ref[...], preferred_element_type=F32)
    y_ref[...] = x_ref[...] + _rms(m, g_ref[...])


def _mix_post(o, x, g, w, tm):
    n, d = x.shape
    row = pl.BlockSpec((tm, d), lambda i: (i, 0))
    return pl.pallas_call(
        _mix_post_kernel,
        grid=(n // tm,),
        in_specs=[row, row, _resident(g.shape), _resident(w.shape)],
        out_specs=row,
        out_shape=jax.ShapeDtypeStruct((n, d), F32),
        compiler_params=_params(("parallel",)),
        name="mix_post",
    )(o, x, g, w)


def _later_matrix(tk, sum_cols=0):
    j = lax.broadcasted_iota(jnp.int32, (tk, tk + sum_cols), 0)
    s = lax.broadcasted_iota(jnp.int32, (tk, tk + sum_cols), 1)
    return jnp.where((j > s) | (s >= tk), 1.0, 0.0).astype(BF16)


def _sb_logits_stage(z, causal):
    sp = _softplus2(z)
    lsig = z - sp
    if causal is not None:
        sp = jnp.where(causal, sp, 0.0)
        lsig = jnp.where(causal, lsig, -jnp.inf)
    return lsig, sp


def _sb_weights_stage(lsig, later, c):
    if c.shape[1] > 1:
        c = jnp.concatenate([c] * (lsig.shape[1] // c.shape[1]), axis=1)
    return jnp.exp2(lsig - later - c)


PAIRS_PER_STEP = 8


def _prompt_attn_kernel(bias_ref, q_ref, kt_ref, v_ref, o_ref, c_ref, acc_ref, *, tq):
    i = pl.program_id(2)
    lane = lax.broadcasted_iota(jnp.int32, (tq, LANES), 1)
    row = lax.broadcasted_iota(jnp.int32, (2 * tq, 1), 0)
    later_mat = _later_matrix(tq)
    lanes_of = lambda g: slice(g * LANES, (g + 1) * LANES)
    rows_of = lambda kt: pl.ds(pl.multiple_of(kt * tq, tq), tq)

    n_heads = bias_ref.shape[0] // 2
    lane2 = lax.broadcasted_iota(jnp.int32, (2 * tq, LANES), 1)
    ones_rows = jnp.where(lax.broadcasted_iota(jnp.int32, (LANES, tq), 0) < 2, 1.0, 0.0).astype(BF16)
    qs = []
    n_pairs = acc_ref.shape[0]
    for g in range(n_pairs):
        q = q_ref[:, lanes_of(g)]
        zero = jnp.zeros_like(q)
        q2 = jnp.concatenate([jnp.where(lane < HEAD_DIM, q, zero), jnp.where(lane >= HEAD_DIM, q, zero)], axis=0)
        head = 2 * (n_pairs * pl.program_id(1) + g)
        hi = jnp.where(row < tq, bias_ref[head], bias_ref[head + 1])
        lo = jnp.where(row < tq, bias_ref[n_heads + head], bias_ref[n_heads + head + 1])
        bias_cols = jnp.where(lane2 == 0, hi, jnp.where(lane2 == 1, lo, 0.0)).astype(BF16)
        qs.append(jnp.concatenate([q2, bias_cols], axis=1))

    def tiles(kt, causal):
        for g in range(n_pairs):
            keys = jnp.concatenate([kt_ref[lanes_of(g), rows_of(kt)], ones_rows], axis=0)
            lsig, sp = _sb_logits_stage(jnp.dot(qs[g], keys, preferred_element_type=F32), causal)
            later = jnp.dot(sp.astype(BF16), later_mat, preferred_element_type=F32)
            c = c_ref[g]
            w = _sb_weights_stage(lsig, later, c)
            c_ref[g] = c + jnp.sum(sp, axis=-1, keepdims=True)
            acc_ref[g] += jnp.dot(w.astype(BF16), v_ref[rows_of(kt), lanes_of(g)], preferred_element_type=F32)

    t = lax.broadcasted_iota(jnp.int32, (2 * tq, tq), 0)
    t = jnp.where(t >= tq, t - tq, t)
    s = lax.broadcasted_iota(jnp.int32, (2 * tq, tq), 1)
    c_ref[...] = jnp.zeros_like(c_ref)
    acc_ref[...] = jnp.zeros_like(acc_ref)
    tiles(i, s < t)
    pl.loop(0, i)(lambda m: tiles(i - 1 - m, None))

    for g in range(n_pairs):
        acc = acc_ref[g]
        o_ref[:, lanes_of(g)] = jnp.where(lane < HEAD_DIM, acc[:tq], acc[tq:]).astype(o_ref.dtype)


def _prompt_attention(q, kt, v, bias, batch, seq, tq):
    n, d = q.shape
    n_pairs = min(PAIRS_PER_STEP, d // LANES)
    width = n_pairs * LANES
    assert d % width == 0
    nq = seq // tq
    q_spec = pl.BlockSpec((tq, width), lambda b, p, i: (b * nq + i, p))
    kt_spec = pl.BlockSpec((None, width, seq), lambda b, p, i: (b, p, 0), pipeline_mode=pl.Buffered(1))
    v_spec = pl.BlockSpec((seq, width), lambda b, p, i: (b, p), pipeline_mode=pl.Buffered(1))
    hi = bias.astype(BF16).astype(F32)
    bias = jnp.concatenate([hi, (bias - hi).astype(BF16).astype(F32)])
    per_pair = lambda cols, dt: pltpu.VMEM((n_pairs, 2 * tq, cols), dt)
    return pl.pallas_call(
        functools.partial(_prompt_attn_kernel, tq=tq),
        grid=(batch, d // width, nq),
        in_specs=[pl.BlockSpec(memory_space=pltpu.SMEM), q_spec, kt_spec, v_spec],
        out_specs=q_spec,
        out_shape=jax.ShapeDtypeStruct((n, d), BF16),
        scratch_shapes=[per_pair(1, F32), per_pair(LANES, F32)],
        compiler_params=_params(("parallel", "parallel", "arbitrary")),
        name="prompt_attn",
    )(bias, q, kt, v)


PAGES_PER_TILE = 2


def _sample_attn_kernel(pt_ref, qbd_ref, bcol_ref, kn_ref, vn_ref, *rest, n_heads, t_new, pps):
    del pt_ref
    k_refs = rest[:pps]
    v_refs = rest[pps:2 * pps]
    o_ref, acc_ref, c_ref = rest[2 * pps:]
    j = pl.program_id(1)
    rows = n_heads * t_new
    page = kn_ref.shape[1]
    qbd = qbd_ref[...]
    bcol = bcol_ref[...]

    def tile(kt, vt, causal):
        z = jnp.dot(qbd, kt, preferred_element_type=F32) + bcol
        lsig, sp = _sb_logits_stage(z, causal)
        tk = kt.shape[1]
        sums = jnp.dot(sp.astype(BF16), _later_matrix(tk, LANES), preferred_element_type=F32)
        c = c_ref[...]
        w = _sb_weights_stage(lsig, sums[:, :tk], c)
        c_ref[...] = c + sums[:, tk:]
        acc_ref[...] += lax.dot_general(w.astype(BF16), vt, (((1,), (1,)), ((), ())), preferred_element_type=F32)

    @pl.when(j == 0)
    def _():
        acc_ref[...] = jnp.zeros_like(acc_ref)
        c_ref[...] = jnp.zeros_like(c_ref)
        t = lax.broadcasted_iota(jnp.int32, (rows, page), 0) % t_new
        s = lax.broadcasted_iota(jnp.int32, (rows, page), 1)
        tile(kn_ref[...], vn_ref[...], s < t)

    for g in range(0, pps, PAGES_PER_TILE):
        group = range(g + PAGES_PER_TILE - 1, g - 1, -1)
        kt = jnp.concatenate([k_refs[r][...].astype(BF16) for r in group], axis=1)
        vt = jnp.concatenate([v_refs[r][...].astype(BF16) for r in group], axis=1)
        tile(kt, vt, None)

    @pl.when(j == pl.num_programs(1) - 1)
    def _():
        d = acc_ref.shape[1]
        head_of_lane = lax.broadcasted_iota(jnp.int32, (t_new, d), 1) // HEAD_DIM
        o = jnp.zeros((t_new, d), F32)
        for h in range(n_heads):
            o = jnp.where(head_of_lane == h, acc_ref[h * t_new:(h + 1) * t_new, :], o)
        o_ref[...] = o.astype(o_ref.dtype)


def _sample_attention(q, kt_new, v_new, bias, kt_pool, vt_pool, page_table, layer, dec_batch, t_new, pps):
    n, d = q.shape
    n_heads = d // HEAD_DIM
    rows = n_heads * t_new
    page = kt_pool.shape[3]
    n_pages = page_table.shape[1]
    assert n_pages % pps == 0 and pps % PAGES_PER_TILE == 0 and t_new <= page
    q4 = q.reshape(dec_batch, t_new, n_heads, HEAD_DIM)
    eye = jnp.eye(n_heads, dtype=q.dtype)
    qbd = jnp.einsum("nthd,hg->nhtgd", q4, eye).reshape(dec_batch, rows, d)
    bcol = jnp.repeat(bias.astype(F32), t_new).reshape(rows, 1)
    pad = ((0, 0), (0, 0), (0, page - t_new))
    kn = jnp.pad(kt_new, pad)
    vn = jnp.pad(v_new.reshape(dec_batch, t_new, d).transpose(0, 2, 1), pad)

    per_seq = lambda shape: pl.BlockSpec((None,) + shape, lambda b, j, pt: (b, 0, 0))

    def page_spec(r):
        return pl.BlockSpec((None, None, d, page),
                            lambda b, j, pt: (layer, pt[b, n_pages - 1 - (j * pps + r)], 0, 0))

    grid_spec = pltpu.PrefetchScalarGridSpec(
        num_scalar_prefetch=1,
        grid=(dec_batch, n_pages // pps),
        in_specs=[per_seq((rows, d)), pl.BlockSpec((rows, 1), lambda b, j, pt: (0, 0)),
                  per_seq((d, page)), per_seq((d, page))]
        + [page_spec(r) for r in range(pps)] * 2,
        out_specs=per_seq((t_new, d)),
        scratch_shapes=[pltpu.VMEM((rows, d), F32), pltpu.VMEM((rows, LANES), F32)],
    )
    o = pl.pallas_call(
        functools.partial(_sample_attn_kernel, n_heads=n_heads, t_new=t_new, pps=pps),
        grid_spec=grid_spec,
        out_shape=jax.ShapeDtypeStruct((dec_batch, t_new, d), BF16),
        compiler_params=_params(("parallel", "arbitrary")),
        name="sample_attn",
    )(page_table, qbd, bcol, kn, vn, *([kt_pool] * pps), *([vt_pool] * pps))
    return o.reshape(n, d)


def _conv_pre_kernel(x_ref, g_ref, w_ref, u_ref, b_ref):
    h = _rms(x_ref[...], g_ref[...]).astype(BF16)
    b_ref[...] = jnp.dot(h, w_ref[0], preferred_element_type=F32)
    c_gate = jnp.dot(h, w_ref[1], preferred_element_type=F32)
    val = jnp.dot(h, w_ref[2], preferred_element_type=F32)
    u_ref[...] = c_gate * val


def _conv_pre(x, g, w3, tm):
    n, d = x.shape
    row = pl.BlockSpec((tm, d), lambda i: (i, 0))
    return pl.pallas_call(
        _conv_pre_kernel,
        grid=(n // tm,),
        in_specs=[row, _resident(g.shape), _resident(w3.shape)],
        out_specs=[row, row],
        out_shape=[jax.ShapeDtypeStruct((n, d), F32)] * 2,
        compiler_params=_params(("parallel",)),
        name="conv_pre",
    )(x, g, w3)


def _conv_post_kernel(u_ref, halo_ref, b_ref, x_ref, cw_ref, g_ref, w_ref, y_ref, *, tiles_per_seq, zero_state):
    u = u_ref[...]
    prev1 = halo_ref[7:8, :]
    prev2 = halo_ref[6:7, :]
    if zero_state:
        first = (pl.program_id(0) % tiles_per_seq) == 0
        prev1 = jnp.where(first, 0.0, prev1)
        prev2 = jnp.where(first, 0.0, prev2)
    row = lax.broadcasted_iota(jnp.int32, u.shape, 0)
    u1 = jnp.where(row == 0, prev1, pltpu.roll(u, 1, 0))
    u2 = jnp.where(row == 0, prev2, jnp.where(row == 1, prev1, pltpu.roll(u, 2, 0)))
    conv = cw_ref[0:1, :] * u2 + cw_ref[1:2, :] * u1 + cw_ref[2:3, :] * u
    y = (b_ref[...] * conv).astype(BF16)
    m = jnp.dot(y, w_ref[...], preferred_element_type=F32)
    y_ref[...] = x_ref[...] + _rms(m, g_ref[...])


def _conv_post(u, halo_src, b, x, conv_w, g, w, tm, seq, zero_state):
    n, d = x.shape
    assert seq % tm == 0 and tm % 8 == 0
    row = pl.BlockSpec((tm, d), lambda i: (i, 0))
    if zero_state:
        halo = pl.BlockSpec((8, d), lambda i: (jnp.maximum(i * (tm // 8) - 1, 0), 0))
    else:
        halo = pl.BlockSpec((8, d), lambda i: (i, 0))
    return pl.pallas_call(
        functools.partial(_conv_post_kernel, tiles_per_seq=seq // tm, zero_state=zero_state),
        grid=(n // tm,),
        in_specs=[row, halo, row, row, _resident(conv_w.shape), _resident(g.shape), _resident(w.shape)],
        out_specs=row,
        out_shape=jax.ShapeDtypeStruct((n, d), F32),
        compiler_params=_params(("parallel",)),
        name="conv_post",
    )(u, halo_src, b, x, conv_w, g, w)


def _row_tile(n, target):
    tm = min(n, target)
    assert n % tm == 0
    return tm


def kernel(x_prompt, x_sample, cache_k, cache_v, state_conv, page_table, norm_gain,
           w_ffn_up, w_ffn_down, w_attn_qkv, w_attn_out, attn_logit_bias, w_conv_in, conv_w, w_conv_out):
    batch, seq, d = x_prompt.shape
    dec_batch, t_new, _ = x_sample.shape
    depth = norm_gain.shape[0]
    d_ff = w_ffn_down.shape[2]
    n_heads = attn_logit_bias.shape[1]
    assert d == n_heads * HEAD_DIM and d % LANES == 0 and d_ff % FF_CHUNK == 0
    n_chunks = d_ff // FF_CHUNK
    n_mixers = 2

    xp = x_prompt.reshape(batch * seq, d)
    xs = x_sample.reshape(dec_batch * t_new, d)
    tm_p = _row_tile(batch * seq, 512)
    tm_s = _row_tile(dec_batch * t_new, 256)
    tq = _row_tile(seq, 256)
    assert seq % tm_p == 0

    wup_r = w_ffn_up.astype(BF16).reshape(depth, 2, d, 2 * n_chunks, FF_CHUNK).transpose(0, 1, 3, 2, 4)
    wdn_r = w_ffn_down.astype(BF16).reshape(depth, 2, n_chunks, FF_CHUNK, d)
    split3 = lambda w: w.astype(BF16).reshape(w.shape[0], d, 3, d).transpose(0, 2, 1, 3)
    wqkv_r = split3(w_attn_qkv)
    wqkv_r = wqkv_r.at[:, 1].set(wqkv_r[:, 1].transpose(0, 2, 1))
    wcin_r = split3(w_conv_in)
    wao = w_attn_out.astype(BF16)
    wco = w_conv_out.astype(BF16)
    to_feature_major = lambda pool: pool.transpose(0, 1, 3, 4, 2).reshape(pool.shape[:2] + (d, pool.shape[2]))
    pool_kt = to_feature_major(cache_k)
    pool_vt = to_feature_major(cache_v)

    outs = {name: [] for name in ("kp", "vp", "cp", "ks", "vs", "cs")}
    for i in range(depth):
        g = norm_gain[i]
        ffn = lambda x, half, tm: _ffn_half(x, g[4 * half:4 * half + 2], wup_r[i, half], wdn_r[i, half], tm)
        xp = ffn(xp, 0, tm_p)
        xs = ffn(xs, 0, tm_s)
        g_pre, g_post = g[2:3], g[3:4]
        if i % n_mixers == 0:
            a = i // n_mixers
            bias = attn_logit_bias[a].astype(F32) * LOG2E
            n_s = dec_batch * t_new
            qp, kpf, vpf, kpb, vpb = _attn_pre(xp, g_pre, wqkv_r[a], tm_p, batch, seq)
            qs, ksf, vsf, ksb, vsb = _attn_pre(xs, g_pre, wqkv_r[a], tm_s, 1, n_s)
            op = _prompt_attention(qp, kpb, vpb, bias, batch, seq, tq)
            per_seq_t = lambda xt: xt.reshape(d, dec_batch, t_new).transpose(1, 0, 2)
            os_ = _sample_attention(qs, per_seq_t(ksb), vsb, bias, pool_kt, pool_vt, page_table, a,
                                    dec_batch, t_new, pps=8)
            xp = _mix_post(op, xp, g_post, wao[a], tm_p)
            xs = _mix_post(os_, xs, g_post, wao[a], tm_s)
            heads_last = lambda xt, b, t: xt.reshape(b, n_heads, HEAD_DIM, t).transpose(0, 3, 1, 2)
            outs["kp"].append(heads_last(kpf, batch, seq)); outs["vp"].append(heads_last(vpf, batch, seq))
            outs["ks"].append(heads_last(per_seq_t(ksf), dec_batch, t_new))
            outs["vs"].append(heads_last(per_seq_t(vsf), dec_batch, t_new))
        else:
            c = i // n_mixers
            n_state = state_conv.shape[2]
            assert n_state == conv_w.shape[1] - 1 == 2 and seq >= n_state and t_new >= n_state and t_new == 8
            up_, bp = _conv_pre(xp, g_pre, wcin_r[c], tm_p)
            us_, bs = _conv_pre(xs, g_pre, wcin_r[c], tm_s)
            xp = _conv_post(up_, up_, bp, xp, conv_w[c], g_post, wco[c], tm_p, seq, zero_state=True)
            halo_s = jnp.pad(state_conv[c], ((0, 0), (8 - n_state, 0), (0, 0))).reshape(dec_batch * 8, d)
            xs = _conv_post(us_, halo_s, bs, xs, conv_w[c], g_post, wco[c], t_new, t_new, zero_state=False)
            outs["cp"].append(up_.reshape(batch, seq, d)[:, seq - n_state:])
            outs["cs"].append(us_.reshape(dec_batch, t_new, d)[:, t_new - n_state:])
        xp = ffn(xp, 1, tm_p)
        xs = ffn(xs, 1, tm_s)

    return (xp.reshape(batch, seq, d), xs.reshape(dec_batch, t_new, d),
            jnp.stack(outs["kp"]), jnp.stack(outs["vp"]), jnp.stack(outs["cp"]),
            jnp.stack(outs["ks"]), jnp.stack(outs["vs"]), jnp.stack(outs["cs"]))
```

```python
import functools

import jax
import jax.numpy as jnp
from jax import lax
from jax.experimental import pallas as pl
from jax.experimental.pallas import tpu as pltpu

F32 = jnp.float32
BF16 = jnp.bfloat16

RMS_EPS = 1e-6
HEAD_DIM = 64
LANES = 128
FF_CHUNK = 256
VMEM_LIMIT = 56 * 1024 * 1024


def _params(sem):
    return pltpu.CompilerParams(dimension_semantics=sem, vmem_limit_bytes=VMEM_LIMIT)


def _resident(shape):
    nd = len(shape)
    return pl.BlockSpec(shape, lambda *_: (0,) * nd, pipeline_mode=pl.Buffered(1))


def _rms(x, g):
    ms = jnp.mean(x * x, axis=-1, keepdims=True)
    return (x * lax.rsqrt(ms + RMS_EPS)) * g


LOG2E = 1.4426950408889634
EXP2_MAX = 126.0


def _softplus2(z):
    return jnp.maximum(z, jnp.log(1.0 + jnp.exp2(jnp.minimum(z, EXP2_MAX))) * LOG2E)


def _ffn_kernel(x_ref, g_ref, wup_ref, wdn_ref, o_ref, *, n_chunks):
    x = x_ref[...]
    h = _rms(x, g_ref[0:1, :]).astype(BF16)
    acc = jnp.zeros(x.shape, F32)
    for j in range(n_chunks):
        gate = jnp.dot(h, wup_ref[j], preferred_element_type=F32)
        up = jnp.dot(h, wup_ref[n_chunks + j], preferred_element_type=F32)
        a = ((gate * jax.nn.sigmoid(gate)) * up).astype(BF16)
        acc = acc + jnp.dot(a, wdn_ref[j], preferred_element_type=F32)
    o_ref[...] = x + 0.5 * _rms(acc, g_ref[1:2, :])


def _ffn_half(x, g2, wup_r, wdn_r, tm):
    n, d = x.shape
    n_chunks = wdn_r.shape[0]
    return pl.pallas_call(
        functools.partial(_ffn_kernel, n_chunks=n_chunks),
        grid=(n // tm,),
        in_specs=[
            pl.BlockSpec((tm, d), lambda i: (i, 0)),
            _resident(g2.shape),
            _resident(wup_r.shape),
            _resident(wdn_r.shape),
        ],
        out_specs=pl.BlockSpec((tm, d), lambda i: (i, 0)),
        out_shape=jax.ShapeDtypeStruct((n, d), F32),
        compiler_params=_params(("parallel",)),
        name="ffn_half",
    )(x, g2, wup_r, wdn_r)


def _attn_pre_kernel(x_ref, g_ref, w_ref, q_ref, ktf_ref, vtf_ref, ktb_ref, vb_ref):
    h = _rms(x_ref[...], g_ref[...]).astype(BF16)
    q = jnp.dot(h, w_ref[0], preferred_element_type=F32)
    q_ref[...] = (q * (HEAD_DIM ** -0.5 * LOG2E)).astype(BF16)
    kt = lax.dot_general(w_ref[1], h, (((1,), (1,)), ((), ())), preferred_element_type=F32)
    ktf_ref[...] = kt
    ktb_ref[...] = kt.astype(BF16)
    v = jnp.dot(h, w_ref[2], preferred_element_type=F32)
    vtf_ref[...] = v.T
    vb_ref[...] = v.astype(BF16)


def _attn_pre(x, g, w3, tm, batch, seq):
    n, d = x.shape
    tiles_per_seq = seq // tm
    assert n == batch * seq and seq % tm == 0
    row = pl.BlockSpec((tm, d), lambda i: (i, 0))
    col = pl.BlockSpec((None, d, tm), lambda i: (i // tiles_per_seq, 0, i % tiles_per_seq))
    feature_major = lambda dt: jax.ShapeDtypeStruct((batch, d, seq), dt)
    return pl.pallas_call(
        _attn_pre_kernel,
        grid=(n // tm,),
        in_specs=[row, _resident(g.shape), _resident(w3.shape)],
        out_specs=[row, col, col, col, row],
        out_shape=[jax.ShapeDtypeStruct((n, d), BF16), feature_major(F32), feature_major(F32),
                   feature_major(BF16), jax.ShapeDtypeStruct((n, d), BF16)],
        compiler_params=_params(("parallel",)),
        name="attn_pre",
    )(x, g, w3)


def _mix_post_kernel(o_ref, x_ref, g_ref, w_ref, y_ref):
    m = jnp.dot(o_ref[...], w_ref[...], preferred_element_type=F32)
    y_ref[...] = x_ref[...] + _rms(m, g_ref[...])


def _mix_post(o, x, g, w, tm):
    n, d = x.shape
    row = pl.BlockSpec((tm, d), lambda i: (i, 0))
    return pl.pallas_call(
        _mix_post_kernel,
        grid=(n // tm,),
        in_specs=[row, row, _resident(g.shape), _resident(w.shape)],
        out_specs=row,
        out_shape=jax.ShapeDtypeStruct((n, d), F32),
        compiler_params=_params(("parallel",)),
        name="mix_post",
    )(o, x, g, w)


def _later_matrix(tk, sum_cols=0):
    j = lax.broadcasted_iota(jnp.int32, (tk, tk + sum_cols), 0)
    s = lax.broadcasted_iota(jnp.int32, (tk, tk + sum_cols), 1)
    return jnp.where((j > s) | (s >= tk), 1.0, 0.0).astype(BF16)


def _sb_logits_stage(z, causal):
    sp = _softplus2(z)
    lsig = z - sp
    if causal is not None:
        sp = jnp.where(causal, sp, 0.0)
        lsig = jnp.where(causal, lsig, -jnp.inf)
    return lsig, sp


def _sb_weights_stage(lsig, later, c):
    if c.shape[1] > 1:
        c = jnp.concatenate([c] * (lsig.shape[1] // c.shape[1]), axis=1)
    return jnp.exp2(lsig - later - c)


PAIRS_PER_STEP = 8


def _prompt_attn_kernel(bias_ref, q_ref, kt_ref, v_ref, o_ref, c_ref, acc_ref, *, tq):
    i = pl.program_id(2)
    lane = lax.broadcasted_iota(jnp.int32, (tq, LANES), 1)
    row = lax.broadcasted_iota(jnp.int32, (2 * tq, 1), 0)
    later_mat = _later_matrix(tq)
    lanes_of = lambda g: slice(g * LANES, (g + 1) * LANES)
    rows_of = lambda kt: pl.ds(pl.multiple_of(kt * tq, tq), tq)

    n_heads = bias_ref.shape[0] // 2
    lane2 = lax.broadcasted_iota(jnp.int32, (2 * tq, LANES), 1)
    ones_rows = jnp.where(lax.broadcasted_iota(jnp.int32, (LANES, tq), 0) < 2, 1.0, 0.0).astype(BF16)
    qs = []
    n_pairs = acc_ref.shape[0]
    for g in range(n_pairs):
        q = q_ref[:, lanes_of(g)]
        zero = jnp.zeros_like(q)
        q2 = jnp.concatenate([jnp.where(lane < HEAD_DIM, q, zero), jnp.where(lane >= HEAD_DIM, q, zero)], axis=0)
        head = 2 * (n_pairs * pl.program_id(1) + g)
        hi = jnp.where(row < tq, bias_ref[head], bias_ref[head + 1])
        lo = jnp.where(row < tq, bias_ref[n_heads + head], bias_ref[n_heads + head + 1])
        bias_cols = jnp.where(lane2 == 0, hi, jnp.where(lane2 == 1, lo, 0.0)).astype(BF16)
        qs.append(jnp.concatenate([q2, bias_cols], axis=1))

    def tiles(kt, causal):
        for g in range(n_pairs):
            keys = jnp.concatenate([kt_ref[lanes_of(g), rows_of(kt)], ones_rows], axis=0)
            lsig, sp = _sb_logits_stage(jnp.dot(qs[g], keys, preferred_element_type=F32), causal)
            later = jnp.dot(sp.astype(BF16), later_mat, preferred_element_type=F32)
            c = c_ref[g]
            w = _sb_weights_stage(lsig, later, c)
            c_ref[g] = c + jnp.sum(sp, axis=-1, keepdims=True)
            acc_ref[g] += jnp.dot(w.astype(BF16), v_ref[rows_of(kt), lanes_of(g)], preferred_element_type=F32)

    t = lax.broadcasted_iota(jnp.int32, (2 * tq, tq), 0)
    t = jnp.where(t >= tq, t - tq, t)
    s = lax.broadcasted_iota(jnp.int32, (2 * tq, tq), 1)
    c_ref[...] = jnp.zeros_like(c_ref)
    acc_ref[...] = jnp.zeros_like(acc_ref)
    tiles(i, s < t)
    pl.loop(0, i)(lambda m: tiles(i - 1 - m, None))

    for g in range(n_pairs):
        acc = acc_ref[g]
        o_ref[:, lanes_of(g)] = jnp.where(lane < HEAD_DIM, acc[:tq], acc[tq:]).astype(o_ref.dtype)


def _prompt_attention(q, kt, v, bias, batch, seq, tq):
    n, d = q.shape
    n_pairs = min(PAIRS_PER_STEP, d // LANES)
    width = n_pairs * LANES
    assert d % width == 0
    nq = seq // tq
    q_spec = pl.BlockSpec((tq, width), lambda b, p, i: (b * nq + i, p))
    kt_spec = pl.BlockSpec((None, width, seq), lambda b, p, i: (b, p, 0), pipeline_mode=pl.Buffered(1))
    v_spec = pl.BlockSpec((seq, width), lambda b, p, i: (b, p), pipeline_mode=pl.Buffered(1))
    hi = bias.astype(BF16).astype(F32)
    bias = jnp.concatenate([hi, (bias - hi).astype(BF16).astype(F32)])
    per_pair = lambda cols, dt: pltpu.VMEM((n_pairs, 2 * tq, cols), dt)
    return pl.pallas_call(
        functools.partial(_prompt_attn_kernel, tq=tq),
        grid=(batch, d // width, nq),
        in_specs=[pl.BlockSpec(memory_space=pltpu.SMEM), q_spec, kt_spec, v_spec],
        out_specs=q_spec,
        out_shape=jax.ShapeDtypeStruct((n, d), BF16),
        scratch_shapes=[per_pair(LANES, F32), per_pair(LANES, F32)],
        compiler_params=_params(("parallel", "parallel", "arbitrary")),
        name="prompt_attn",
    )(bias, q, kt, v)


PAGES_PER_TILE = 2


def _sample_attn_kernel(pt_ref, qbd_ref, bcol_ref, kn_ref, vn_ref, *rest, n_heads, t_new, pps):
    del pt_ref
    k_refs = rest[:pps]
    v_refs = rest[pps:2 * pps]
    o_ref, acc_ref, c_ref = rest[2 * pps:]
    j = pl.program_id(1)
    rows = n_heads * t_new
    page = kn_ref.shape[1]
    qbd = qbd_ref[...]
    bcol = bcol_ref[...]

    def tiles(kt, vt, tk, causal):
        n = kt.shape[1] // tk
        z = jnp.dot(qbd, kt, preferred_element_type=F32) + bcol
        lsig, sp = _sb_logits_stage(z, causal)
        stacked = jnp.concatenate([sp[:, a * tk:(a + 1) * tk] for a in range(n)], axis=0).astype(BF16)
        sums = jnp.dot(stacked, _later_matrix(tk, LANES), preferred_element_type=F32)
        of_tile = lambda a: sums[a * rows:(a + 1) * rows]
        later = jnp.concatenate([of_tile(a)[:, :tk] for a in range(n)], axis=1)
        c, carries = c_ref[...], [None] * n
        for a in reversed(range(n)):
            carries[a] = c
            c = c + of_tile(a)[:, tk:]
        c_ref[...] = c
        w = _sb_weights_stage(lsig, later, jnp.concatenate([x for x in carries for _ in range(tk // LANES)], axis=1))
        acc_ref[...] += lax.dot_general(w.astype(BF16), vt, (((1,), (1,)), ((), ())), preferred_element_type=F32)

    @pl.when(j == 0)
    def _():
        acc_ref[...] = jnp.zeros_like(acc_ref)
        c_ref[...] = jnp.zeros_like(c_ref)
        t = lax.broadcasted_iota(jnp.int32, (rows, page), 0) % t_new
        s = lax.broadcasted_iota(jnp.int32, (rows, page), 1)
        tiles(kn_ref[...], vn_ref[...], page, s < t)

    ascending = range(pps - 1, -1, -1)
    kt = jnp.concatenate([k_refs[r][...].astype(BF16) for r in ascending], axis=1)
    vt = jnp.concatenate([v_refs[r][...].astype(BF16) for r in ascending], axis=1)
    tiles(kt, vt, PAGES_PER_TILE * page, None)

    @pl.when(j == pl.num_programs(1) - 1)
    def _():
        d = acc_ref.shape[1]
        head_of_lane = lax.broadcasted_iota(jnp.int32, (t_new, d), 1) // HEAD_DIM
        o = jnp.zeros((t_new, d), F32)
        for h in range(n_heads):
            o = jnp.where(head_of_lane == h, acc_ref[h * t_new:(h + 1) * t_new, :], o)
        o_ref[...] = o.astype(o_ref.dtype)


def _sample_attention(q, kt_new, v_new, bias, kt_pool, vt_pool, page_table, layer, dec_batch, t_new, pps):
    n, d = q.shape
    n_heads = d // HEAD_DIM
    rows = n_heads * t_new
    page = kt_pool.shape[3]
    n_pages = page_table.shape[1]
    assert n_pages % pps == 0 and pps % PAGES_PER_TILE == 0 and t_new <= page
    q4 = q.reshape(dec_batch, t_new, n_heads, HEAD_DIM)
    eye = jnp.eye(n_heads, dtype=q.dtype)
    qbd = jnp.einsum("nthd,hg->nhtgd", q4, eye).reshape(dec_batch, rows, d)
    bcol = jnp.repeat(bias.astype(F32), t_new).reshape(rows, 1)
    pad = ((0, 0), (0, 0), (0, page - t_new))
    kn = jnp.pad(kt_new, pad)
    vn = jnp.pad(v_new.reshape(dec_batch, t_new, d).transpose(0, 2, 1), pad)

    per_seq = lambda shape: pl.BlockSpec((None,) + shape, lambda b, j, pt: (b, 0, 0))

    def page_spec(r):
        return pl.BlockSpec((None, None, d, page),
                            lambda b, j, pt: (layer, pt[b, n_pages - 1 - (j * pps + r)], 0, 0))

    grid_spec = pltpu.PrefetchScalarGridSpec(
        num_scalar_prefetch=1,
        grid=(dec_batch, n_pages // pps),
        in_specs=[per_seq((rows, d)), pl.BlockSpec((rows, 1), lambda b, j, pt: (0, 0)),
                  per_seq((d, page)), per_seq((d, page))]
        + [page_spec(r) for r in range(pps)] * 2,
        out_specs=per_seq((t_new, d)),
        scratch_shapes=[pltpu.VMEM((rows, d), F32), pltpu.VMEM((rows, LANES), F32)],
    )
    o = pl.pallas_call(
        functools.partial(_sample_attn_kernel, n_heads=n_heads, t_new=t_new, pps=pps),
        grid_spec=grid_spec,
        out_shape=jax.ShapeDtypeStruct((dec_batch, t_new, d), BF16),
        compiler_params=_params(("parallel", "arbitrary")),
        name="sample_attn",
    )(page_table, qbd, bcol, kn, vn, *([kt_pool] * pps), *([vt_pool] * pps))
    return o.reshape(n, d)


def _conv_pre_kernel(x_ref, g_ref, w_ref, u_ref, b_ref):
    h = _rms(x_ref[...], g_ref[...]).astype(BF16)
    b_ref[...] = jnp.dot(h, w_ref[0], preferred_element_type=F32)
    c_gate = jnp.dot(h, w_ref[1], preferred_element_type=F32)
    val = jnp.dot(h, w_ref[2], preferred_element_type=F32)
    u_ref[...] = c_gate * val


def _conv_pre(x, g, w3, tm):
    n, d = x.shape
    row = pl.BlockSpec((tm, d), lambda i: (i, 0))
    return pl.pallas_call(
        _conv_pre_kernel,
        grid=(n // tm,),
        in_specs=[row, _resident(g.shape), _resident(w3.shape)],
        out_specs=[row, row],
        out_shape=[jax.ShapeDtypeStruct((n, d), F32)] * 2,
        compiler_params=_params(("parallel",)),
        name="conv_pre",
    )(x, g, w3)


def _conv_post_kernel(u_ref, halo_ref, b_ref, x_ref, cw_ref, g_ref, w_ref, y_ref, *, tiles_per_seq, zero_state):
    u = u_ref[...]
    prev1 = halo_ref[7:8, :]
    prev2 = halo_ref[6:7, :]
    if zero_state:
        first = (pl.program_id(0) % tiles_per_seq) == 0
        prev1 = jnp.where(first, 0.0, prev1)
        prev2 = jnp.where(first, 0.0, prev2)
    row = lax.broadcasted_iota(jnp.int32, u.shape, 0)
    u1 = jnp.where(row == 0, prev1, pltpu.roll(u, 1, 0))
    u2 = jnp.where(row == 0, prev2, jnp.where(row == 1, prev1, pltpu.roll(u, 2, 0)))
    conv = cw_ref[0:1, :] * u2 + cw_ref[1:2, :] * u1 + cw_ref[2:3, :] * u
    y = (b_ref[...] * conv).astype(BF16)
    m = jnp.dot(y, w_ref[...], preferred_element_type=F32)
    y_ref[...] = x_ref[...] + _rms(m, g_ref[...])


def _conv_post(u, halo_src, b, x, conv_w, g, w, tm, seq, zero_state):
    n, d = x.shape
    assert seq % tm == 0 and tm % 8 == 0
    row = pl.BlockSpec((tm, d), lambda i: (i, 0))
    if zero_state:
        halo = pl.BlockSpec((8, d), lambda i: (jnp.maximum(i * (tm // 8) - 1, 0), 0))
    else:
        halo = pl.BlockSpec((8, d), lambda i: (i, 0))
    return pl.pallas_call(
        functools.partial(_conv_post_kernel, tiles_per_seq=seq // tm, zero_state=zero_state),
        grid=(n // tm,),
        in_specs=[row, halo, row, row, _resident(conv_w.shape), _resident(g.shape), _resident(w.shape)],
        out_specs=row,
        out_shape=jax.ShapeDtypeStruct((n, d), F32),
        compiler_params=_params(("parallel",)),
        name="conv_post",
    )(u, halo_src, b, x, conv_w, g, w)


def _row_tile(n, target):
    tm = min(n, target)
    assert n % tm == 0
    return tm


def kernel(x_prompt, x_sample, cache_k, cache_v, state_conv, page_table, norm_gain,
           w_ffn_up, w_ffn_down, w_attn_qkv, w_attn_out, attn_logit_bias, w_conv_in, conv_w, w_conv_out):
    batch, seq, d = x_prompt.shape
    dec_batch, t_new, _ = x_sample.shape
    depth = norm_gain.shape[0]
    d_ff = w_ffn_down.shape[2]
    n_heads = attn_logit_bias.shape[1]
    assert d == n_heads * HEAD_DIM and d % LANES == 0 and d_ff % FF_CHUNK == 0
    n_chunks = d_ff // FF_CHUNK
    n_mixers = 2

    xp = x_prompt.reshape(batch * seq, d)
    xs = x_sample.reshape(dec_batch * t_new, d)
    tm_p = _row_tile(batch * seq, 512)
    tm_s = _row_tile(dec_batch * t_new, 256)
    tq = _row_tile(seq, 256)
    assert seq % tm_p == 0

    wup_r = w_ffn_up.astype(BF16).reshape(depth, 2, d, 2 * n_chunks, FF_CHUNK).transpose(0, 1, 3, 2, 4)
    wdn_r = w_ffn_down.astype(BF16).reshape(depth, 2, n_chunks, FF_CHUNK, d)
    split3 = lambda w: w.astype(BF16).reshape(w.shape[0], d, 3, d).transpose(0, 2, 1, 3)
    wqkv_r = split3(w_attn_qkv)
    wqkv_r = wqkv_r.at[:, 1].set(wqkv_r[:, 1].transpose(0, 2, 1))
    wcin_r = split3(w_conv_in)
    wao = w_attn_out.astype(BF16)
    wco = w_conv_out.astype(BF16)
    to_feature_major = lambda pool: pool.transpose(0, 1, 3, 4, 2).reshape(pool.shape[:2] + (d, pool.shape[2]))
    pool_kt = to_feature_major(cache_k)
    pool_vt = to_feature_major(cache_v)

    outs = {name: [] for name in ("kp", "vp", "cp", "ks", "vs", "cs")}
    for i in range(depth):
        g = norm_gain[i]
        ffn = lambda x, half, tm: _ffn_half(x, g[4 * half:4 * half + 2], wup_r[i, half], wdn_r[i, half], tm)
        xp = ffn(xp, 0, tm_p)
        xs = ffn(xs, 0, tm_s)
        g_pre, g_post = g[2:3], g[3:4]
        if i % n_mixers == 0:
            a = i // n_mixers
            bias = attn_logit_bias[a].astype(F32) * LOG2E
            n_s = dec_batch * t_new
            qp, kpf, vpf, kpb, vpb = _attn_pre(xp, g_pre, wqkv_r[a], tm_p, batch, seq)
            qs, ksf, vsf, ksb, vsb = _attn_pre(xs, g_pre, wqkv_r[a], tm_s, 1, n_s)
            op = _prompt_attention(qp, kpb, vpb, bias, batch, seq, tq)
            per_seq_t = lambda xt: xt.reshape(d, dec_batch, t_new).transpose(1, 0, 2)
            os_ = _sample_attention(qs, per_seq_t(ksb), vsb, bias, pool_kt, pool_vt, page_table, a,
                                    dec_batch, t_new, pps=8)
            xp = _mix_post(op, xp, g_post, wao[a], tm_p)
            xs = _mix_post(os_, xs, g_post, wao[a], tm_s)
            heads_last = lambda xt, b, t: xt.reshape(b, n_heads, HEAD_DIM, t).transpose(0, 3, 1, 2)
            outs["kp"].append(heads_last(kpf, batch, seq)); outs["vp"].append(heads_last(vpf, batch, seq))
            outs["ks"].append(heads_last(per_seq_t(ksf), dec_batch, t_new))
            outs["vs"].append(heads_last(per_seq_t(vsf), dec_batch, t_new))
        else:
            c = i // n_mixers
            n_state = state_conv.shape[2]
            assert n_state == conv_w.shape[1] - 1 == 2 and seq >= n_state and t_new >= n_state and t_new == 8
            up_, bp = _conv_pre(xp, g_pre, wcin_r[c], tm_p)
            us_, bs = _conv_pre(xs, g_pre, wcin_r[c], tm_s)
            xp = _conv_post(up_, up_, bp, xp, conv_w[c], g_post, wco[c], tm_p, seq, zero_state=True)
            halo_s = jnp.pad(state_conv[c], ((0, 0), (8 - n_state, 0), (0, 0))).reshape(dec_batch * 8, d)
            xs = _conv_post(us_, halo_s, bs, xs, conv_w[c], g_post, wco[c], t_new, t_new, zero_state=False)
            outs["cp"].append(up_.reshape(batch, seq, d)[:, seq - n_state:])
            outs["cs"].append(us_.reshape(dec_batch, t_new, d)[:, t_new - n_state:])
        xp = ffn(xp, 1, tm_p)
        xs = ffn(xs, 1, tm_s)

    return (xp.reshape(batch, seq, d), xs.reshape(dec_batch, t_new, d),
            jnp.stack(outs["kp"]), jnp.stack(outs["vp"]), jnp.stack(outs["cp"]),
            jnp.stack(outs["ks"]), jnp.stack(outs["vs"]), jnp.stack(outs["cs"]))
```

```python
import functools

import jax
import jax.numpy as jnp
from jax import lax
from jax.experimental import pallas as pl
from jax.experimental.pallas import tpu as pltpu

F32 = jnp.float32
BF16 = jnp.bfloat16

RMS_EPS = 1e-6
HEAD_DIM = 64
LANES = 128
FF_CHUNK = 256
HALO_ROWS = 8
VMEM_LIMIT = 56 * 1024 * 1024


def _params(sem):
    return pltpu.CompilerParams(dimension_semantics=sem, vmem_limit_bytes=VMEM_LIMIT)


def _resident(shape):
    nd = len(shape)
    return pl.BlockSpec(shape, lambda *_: (0,) * nd, pipeline_mode=pl.Buffered(1))


def _rms(x, g):
    ms = jnp.mean(x * x, axis=-1, keepdims=True)
    return (x * lax.rsqrt(ms + RMS_EPS)) * g


LOG2E = 1.4426950408889634
EXP2_MAX = 126.0


def _softplus2(z):
    return jnp.maximum(z, jnp.log(1.0 + jnp.exp2(jnp.minimum(z, EXP2_MAX))) * LOG2E)


def _ffn_kernel(*refs, d_ff, mixed):
    if mixed:
        x_ref, g_ref, wup_ref, wdn_ref, om_ref, gm_ref, wm_ref, o_ref = refs
        x = x_ref[...] + _rms(jnp.dot(om_ref[...], wm_ref[...], preferred_element_type=F32), gm_ref[...])
    else:
        x_ref, g_ref, wup_ref, wdn_ref, o_ref = refs
        x = x_ref[...]
    h = _rms(x, g_ref[0:1, :]).astype(BF16)
    acc = jnp.zeros(x.shape, F32)
    for lo in range(0, d_ff, FF_CHUNK):
        hi = lo + FF_CHUNK
        gate = jnp.dot(h, wup_ref[:, lo:hi], preferred_element_type=F32)
        up = jnp.dot(h, wup_ref[:, d_ff + lo:d_ff + hi], preferred_element_type=F32)
        a = ((gate * jax.nn.sigmoid(gate)) * up).astype(BF16)
        acc = acc + jnp.dot(a, wdn_ref[lo:hi, :], preferred_element_type=F32)
    o_ref[...] = x + 0.5 * _rms(acc, g_ref[1:2, :])


def _ffn_half(x, g2, wup, wdn, tm, mix=None):
    n, d = x.shape
    row = pl.BlockSpec((tm, d), lambda i: (i, 0))
    operands, specs = [x, g2, wup, wdn], [row, _resident(g2.shape), _resident(wup.shape), _resident(wdn.shape)]
    if mix is not None:
        o, g_mix, w_mix = mix
        operands += [o, g_mix, w_mix]
        specs += [row, _resident(g_mix.shape), _resident(w_mix.shape)]
    return pl.pallas_call(
        functools.partial(_ffn_kernel, d_ff=wdn.shape[0], mixed=mix is not None),
        grid=(n // tm,),
        in_specs=specs,
        out_specs=row,
        out_shape=jax.ShapeDtypeStruct((n, d), F32),
        compiler_params=_params(("parallel",)),
        name="ffn_half",
    )(*operands)


def _attn_pre_kernel(x_ref, g_ref, w_ref, wkt_ref, q_ref, ktf_ref, vtf_ref, ktb_ref, vb_ref):
    d = x_ref.shape[1]
    h = _rms(x_ref[...], g_ref[...]).astype(BF16)
    q = jnp.dot(h, w_ref[:, :d], preferred_element_type=F32)
    q_ref[...] = (q * (HEAD_DIM ** -0.5 * LOG2E)).astype(BF16)
    kt = lax.dot_general(wkt_ref[...], h, (((1,), (1,)), ((), ())), preferred_element_type=F32)
    ktf_ref[...] = kt
    ktb_ref[...] = kt.astype(BF16)
    v = jnp.dot(h, w_ref[:, 2 * d:], preferred_element_type=F32)
    vtf_ref[...] = v.T
    vb_ref[...] = v.astype(BF16)


def _attn_pre(x, g, w, wkt, tm, batch, seq):
    n, d = x.shape
    tiles_per_seq = seq // tm
    assert n == batch * seq and seq % tm == 0
    row = pl.BlockSpec((tm, d), lambda i: (i, 0))
    col = pl.BlockSpec((None, d, tm), lambda i: (i // tiles_per_seq, 0, i % tiles_per_seq))
    feature_major = lambda dt: jax.ShapeDtypeStruct((batch, d, seq), dt)
    return pl.pallas_call(
        _attn_pre_kernel,
        grid=(n // tm,),
        in_specs=[row, _resident(g.shape), _resident(w.shape), _resident(wkt.shape)],
        out_specs=[row, col, col, col, row],
        out_shape=[jax.ShapeDtypeStruct((n, d), BF16), feature_major(F32), feature_major(F32),
                   feature_major(BF16), jax.ShapeDtypeStruct((n, d), BF16)],
        compiler_params=_params(("parallel",)),
        name="attn_pre",
    )(x, g, w, wkt)


def _mix_post_kernel(o_ref, x_ref, g_ref, w_ref, y_ref):
    m = jnp.dot(o_ref[...], w_ref[...], preferred_element_type=F32)
    y_ref[...] = x_ref[...] + _rms(m, g_ref[...])


def _mix_post(o, x, g, w, tm):
    n, d = x.shape
    row = pl.BlockSpec((tm, d), lambda i: (i, 0))
    return pl.pallas_call(
        _mix_post_kernel,
        grid=(n // tm,),
        in_specs=[row, row, _resident(g.shape), _resident(w.shape)],
        out_specs=row,
        out_shape=jax.ShapeDtypeStruct((n, d), F32),
        compiler_params=_params(("parallel",)),
        name="mix_post",
    )(o, x, g, w)


def _later_matrix(tk, sum_cols=0):
    j = lax.broadcasted_iota(jnp.int32, (tk, tk + sum_cols), 0)
    s = lax.broadcasted_iota(jnp.int32, (tk, tk + sum_cols), 1)
    return jnp.where((j > s) | (s >= tk), 1.0, 0.0).astype(BF16)


def _sb_logits_stage(z, causal):
    sp = _softplus2(z)
    lsig = z - sp
    if causal is not None:
        sp = jnp.where(causal, sp, 0.0)
        lsig = jnp.where(causal, lsig, -jnp.inf)
    return lsig, sp


def _sb_weights_stage(lsig, later, c):
    if c.shape[1] > 1:
        c = jnp.concatenate([c] * (lsig.shape[1] // c.shape[1]), axis=1)
    return jnp.exp2(lsig - later - c)


PAIRS_PER_STEP = 8


def _prompt_attn_kernel(bias_ref, q_ref, kt_ref, v_ref, o_ref, c_ref, acc_ref, *, tq):
    i = pl.program_id(2)
    lane = lax.broadcasted_iota(jnp.int32, (tq, LANES), 1)
    row = lax.broadcasted_iota(jnp.int32, (2 * tq, 1), 0)
    later_mat = _later_matrix(tq)
    lanes_of = lambda g: slice(g * LANES, (g + 1) * LANES)
    rows_of = lambda kt: pl.ds(pl.multiple_of(kt * tq, tq), tq)

    n_heads = bias_ref.shape[0] // 2
    lane2 = lax.broadcasted_iota(jnp.int32, (2 * tq, LANES), 1)
    ones_rows = jnp.where(lax.broadcasted_iota(jnp.int32, (LANES, tq), 0) < 2, 1.0, 0.0).astype(BF16)
    qs = []
    n_pairs = acc_ref.shape[0]
    for g in range(n_pairs):
        q = q_ref[:, lanes_of(g)]
        zero = jnp.zeros_like(q)
        q2 = jnp.concatenate([jnp.where(lane < HEAD_DIM, q, zero), jnp.where(lane >= HEAD_DIM, q, zero)], axis=0)
        head = 2 * (n_pairs * pl.program_id(1) + g)
        hi = jnp.where(row < tq, bias_ref[head], bias_ref[head + 1])
        lo = jnp.where(row < tq, bias_ref[n_heads + head], bias_ref[n_heads + head + 1])
        bias_cols = jnp.where(lane2 == 0, hi, jnp.where(lane2 == 1, lo, 0.0)).astype(BF16)
        qs.append(jnp.concatenate([q2, bias_cols], axis=1))

    def tiles(kt, causal):
        for g in range(n_pairs):
            keys = jnp.concatenate([kt_ref[lanes_of(g), rows_of(kt)], ones_rows], axis=0)
            lsig, sp = _sb_logits_stage(jnp.dot(qs[g], keys, preferred_element_type=F32), causal)
            later = jnp.dot(sp.astype(BF16), later_mat, preferred_element_type=F32)
            c = c_ref[g]
            w = _sb_weights_stage(lsig, later, c)
            c_ref[g] = c + jnp.sum(sp, axis=-1, keepdims=True)
            acc_ref[g] += jnp.dot(w.astype(BF16), v_ref[rows_of(kt), lanes_of(g)], preferred_element_type=F32)

    t = lax.broadcasted_iota(jnp.int32, (2 * tq, tq), 0)
    t = jnp.where(t >= tq, t - tq, t)
    s = lax.broadcasted_iota(jnp.int32, (2 * tq, tq), 1)
    c_ref[...] = jnp.zeros_like(c_ref)
    acc_ref[...] = jnp.zeros_like(acc_ref)
    tiles(i, s < t)
    pl.loop(0, i)(lambda m: tiles(i - 1 - m, None))

    for g in range(n_pairs):
        acc = acc_ref[g]
        o_ref[:, lanes_of(g)] = jnp.where(lane < HEAD_DIM, acc[:tq], acc[tq:]).astype(o_ref.dtype)


def _prompt_attention(q, kt, v, bias, batch, seq, tq):
    n, d = q.shape
    n_pairs = min(PAIRS_PER_STEP, d // LANES)
    width = n_pairs * LANES
    assert d % width == 0
    nq = seq // tq
    q_spec = pl.BlockSpec((tq, width), lambda b, p, i: (b * nq + i, p))
    kt_spec = pl.BlockSpec((None, width, seq), lambda b, p, i: (b, p, 0), pipeline_mode=pl.Buffered(1))
    v_spec = pl.BlockSpec((seq, width), lambda b, p, i: (b, p), pipeline_mode=pl.Buffered(1))
    hi = bias.astype(BF16).astype(F32)
    bias = jnp.concatenate([hi, (bias - hi).astype(BF16).astype(F32)])
    per_pair = lambda cols, dt: pltpu.VMEM((n_pairs, 2 * tq, cols), dt)
    return pl.pallas_call(
        functools.partial(_prompt_attn_kernel, tq=tq),
        grid=(batch, d // width, nq),
        in_specs=[pl.BlockSpec(memory_space=pltpu.SMEM), q_spec, kt_spec, v_spec],
        out_specs=q_spec,
        out_shape=jax.ShapeDtypeStruct((n, d), BF16),
        scratch_shapes=[per_pair(LANES, F32), per_pair(LANES, F32)],
        compiler_params=_params(("parallel", "parallel", "arbitrary")),
        name="prompt_attn",
    )(bias, q, kt, v)


PAGES_PER_TILE = 2
PAGES_PER_STEP = 16


def _sample_attn_kernel(pt_ref, qbd_ref, bcol_ref, kn_ref, vn_ref, *rest, n_heads, t_new, pps):
    del pt_ref
    k_refs = rest[:pps]
    v_refs = rest[pps:2 * pps]
    o_ref, acc_ref, c_ref = rest[2 * pps:]
    j = pl.program_id(1)
    rows = n_heads * t_new
    page = kn_ref.shape[1]
    qbd = qbd_ref[...]
    bcol = bcol_ref[...]

    def tiles(kt, vt, tk, causal):
        n = kt.shape[1] // tk
        z = jnp.dot(qbd, kt, preferred_element_type=F32) + bcol
        lsig, sp = _sb_logits_stage(z, causal)
        stacked = jnp.concatenate([sp[:, a * tk:(a + 1) * tk] for a in range(n)], axis=0).astype(BF16)
        sums = jnp.dot(stacked, _later_matrix(tk, LANES), preferred_element_type=F32)
        of_tile = lambda a: sums[a * rows:(a + 1) * rows]
        later = jnp.concatenate([of_tile(a)[:, :tk] for a in range(n)], axis=1)
        c, carries = c_ref[...], [None] * n
        for a in reversed(range(n)):
            carries[a] = c
            c = c + of_tile(a)[:, tk:]
        c_ref[...] = c
        w = _sb_weights_stage(lsig, later, jnp.concatenate([x for x in carries for _ in range(tk // LANES)], axis=1))
        acc_ref[...] += lax.dot_general(w.astype(BF16), vt, (((1,), (1,)), ((), ())), preferred_element_type=F32)

    @pl.when(j == 0)
    def _():
        acc_ref[...] = jnp.zeros_like(acc_ref)
        c_ref[...] = jnp.zeros_like(c_ref)
        t = lax.broadcasted_iota(jnp.int32, (rows, page), 0) % t_new
        s = lax.broadcasted_iota(jnp.int32, (rows, page), 1)
        tiles(kn_ref[...], vn_ref[...], page, s < t)

    ascending = range(pps - 1, -1, -1)
    kt = jnp.concatenate([k_refs[r][...].astype(BF16) for r in ascending], axis=1)
    vt = jnp.concatenate([v_refs[r][...].astype(BF16) for r in ascending], axis=1)
    tiles(kt, vt, PAGES_PER_TILE * page, None)

    @pl.when(j == pl.num_programs(1) - 1)
    def _():
        d = acc_ref.shape[1]
        head_of_lane = lax.broadcasted_iota(jnp.int32, (t_new, d), 1) // HEAD_DIM
        o = jnp.zeros((t_new, d), F32)
        for h in range(n_heads):
            o = jnp.where(head_of_lane == h, acc_ref[h * t_new:(h + 1) * t_new, :], o)
        o_ref[...] = o.astype(o_ref.dtype)


def _sample_attention(q, kt_new, v_new, bias, kt_pool, vt_pool, page_table, layer, dec_batch, t_new, pps):
    n, d = q.shape
    n_heads = d // HEAD_DIM
    rows = n_heads * t_new
    page = kt_pool.shape[3]
    n_pages = page_table.shape[1]
    assert n_pages % pps == 0 and pps % PAGES_PER_TILE == 0 and t_new <= page
    q4 = q.reshape(dec_batch, t_new, n_heads, HEAD_DIM)
    eye = jnp.eye(n_heads, dtype=q.dtype)
    qbd = jnp.einsum("nthd,hg->nhtgd", q4, eye).reshape(dec_batch, rows, d)
    bcol = jnp.repeat(bias.astype(F32), t_new).reshape(rows, 1)
    pad = ((0, 0), (0, 0), (0, page - t_new))
    kn = jnp.pad(kt_new, pad)
    vn = jnp.pad(v_new.reshape(dec_batch, t_new, d).transpose(0, 2, 1), pad)

    per_seq = lambda shape: pl.BlockSpec((None,) + shape, lambda b, j, pt: (b, 0, 0))

    def page_spec(r):
        return pl.BlockSpec((None, None, d, page),
                            lambda b, j, pt: (layer, pt[b, n_pages - 1 - (j * pps + r)], 0, 0))

    grid_spec = pltpu.PrefetchScalarGridSpec(
        num_scalar_prefetch=1,
        grid=(dec_batch, n_pages // pps),
        in_specs=[per_seq((rows, d)), pl.BlockSpec((rows, 1), lambda b, j, pt: (0, 0)),
                  per_seq((d, page)), per_seq((d, page))]
        + [page_spec(r) for r in range(pps)] * 2,
        out_specs=per_seq((t_new, d)),
        scratch_shapes=[pltpu.VMEM((rows, d), F32), pltpu.VMEM((rows, LANES), F32)],
    )
    o = pl.pallas_call(
        functools.partial(_sample_attn_kernel, n_heads=n_heads, t_new=t_new, pps=pps),
        grid_spec=grid_spec,
        out_shape=jax.ShapeDtypeStruct((dec_batch, t_new, d), BF16),
        compiler_params=_params(("parallel", "arbitrary")),
        name="sample_attn",
    )(page_table, qbd, bcol, kn, vn, *([kt_pool] * pps), *([vt_pool] * pps))
    return o.reshape(n, d)


def _conv_pre_kernel(x_ref, g_ref, w_ref, u_ref, b_ref, *tail_ref):
    d = x_ref.shape[1]
    h = _rms(x_ref[...], g_ref[...]).astype(BF16)
    b_ref[...] = jnp.dot(h, w_ref[:, :d], preferred_element_type=F32).astype(b_ref.dtype)
    c_gate = jnp.dot(h, w_ref[:, d:2 * d], preferred_element_type=F32)
    val = jnp.dot(h, w_ref[:, 2 * d:], preferred_element_type=F32)
    u = c_gate * val
    u_ref[...] = u.astype(u_ref.dtype)
    if tail_ref:
        tail_ref[0][...] = u[u.shape[0] - HALO_ROWS:]


def _conv_pre(x, g, w, tm, narrow):
    n, d = x.shape
    row = pl.BlockSpec((tm, d), lambda i: (i, 0))
    dt = BF16 if narrow else F32
    out_specs, out_shape = [row, row], [jax.ShapeDtypeStruct((n, d), dt)] * 2
    if narrow:
        out_specs.append(pl.BlockSpec((None, HALO_ROWS, d), lambda i: (i, 0, 0)))
        out_shape.append(jax.ShapeDtypeStruct((n // tm, HALO_ROWS, d), F32))
    return pl.pallas_call(
        _conv_pre_kernel,
        grid=(n // tm,),
        in_specs=[row, _resident(g.shape), _resident(w.shape)],
        out_specs=out_specs,
        out_shape=out_shape,
        compiler_params=_params(("parallel",)),
        name="conv_pre",
    )(x, g, w)


def _conv_post_kernel(u_ref, halo_ref, b_ref, x_ref, cw_ref, g_ref, w_ref, y_ref, *, tiles_per_seq, zero_state):
    u = u_ref[...].astype(F32)
    prev1 = halo_ref[HALO_ROWS - 1:HALO_ROWS, :]
    prev2 = halo_ref[HALO_ROWS - 2:HALO_ROWS - 1, :]
    if zero_state:
        first = (pl.program_id(0) % tiles_per_seq) == 0
        prev1 = jnp.where(first, 0.0, prev1)
        prev2 = jnp.where(first, 0.0, prev2)
    row = lax.broadcasted_iota(jnp.int32, u.shape, 0)
    u1 = jnp.where(row == 0, prev1, pltpu.roll(u, 1, 0))
    u2 = jnp.where(row == 0, prev2, jnp.where(row == 1, prev1, pltpu.roll(u, 2, 0)))
    conv = cw_ref[0:1, :] * u2 + cw_ref[1:2, :] * u1 + cw_ref[2:3, :] * u
    y = (b_ref[...].astype(F32) * conv).astype(BF16)
    m = jnp.dot(y, w_ref[...], preferred_element_type=F32)
    y_ref[...] = x_ref[...] + _rms(m, g_ref[...])


def _conv_post(u, halo_src, b, x, conv_w, g, w, tm, seq, zero_state):
    n, d = x.shape
    assert seq % tm == 0 and tm % HALO_ROWS == 0
    row = pl.BlockSpec((tm, d), lambda i: (i, 0))
    if zero_state:
        halo = pl.BlockSpec((HALO_ROWS, d), lambda i: (jnp.maximum(i - 1, 0), 0))
    else:
        halo = pl.BlockSpec((HALO_ROWS, d), lambda i: (i, 0))
    return pl.pallas_call(
        functools.partial(_conv_post_kernel, tiles_per_seq=seq // tm, zero_state=zero_state),
        grid=(n // tm,),
        in_specs=[row, halo, row, row, _resident(conv_w.shape), _resident(g.shape), _resident(w.shape)],
        out_specs=row,
        out_shape=jax.ShapeDtypeStruct((n, d), F32),
        compiler_params=_params(("parallel",)),
        name="conv_post",
    )(u, halo_src, b, x, conv_w, g, w)


def _row_tile(n, target):
    tm = min(n, target)
    assert n % tm == 0
    return tm


def kernel(x_prompt, x_sample, cache_k, cache_v, state_conv, page_table, norm_gain,
           w_ffn_up, w_ffn_down, w_attn_qkv, w_attn_out, attn_logit_bias, w_conv_in, conv_w, w_conv_out):
    batch, seq, d = x_prompt.shape
    dec_batch, t_new, _ = x_sample.shape
    depth = norm_gain.shape[0]
    d_ff = w_ffn_down.shape[2]
    n_heads = attn_logit_bias.shape[1]
    assert d == n_heads * HEAD_DIM and d % LANES == 0 and d_ff % FF_CHUNK == 0
    n_mixers = 2

    xp = x_prompt.reshape(batch * seq, d)
    xs = x_sample.reshape(dec_batch * t_new, d)
    tm_p = _row_tile(batch * seq, 512)
    tm_s = _row_tile(dec_batch * t_new, 256)
    tq = _row_tile(seq, 256)
    assert seq % tm_p == 0

    wup, wdn = w_ffn_up.astype(BF16), w_ffn_down.astype(BF16)
    wqkv, wcin = w_attn_qkv.astype(BF16), w_conv_in.astype(BF16)
    wkt = wqkv[:, :, d:2 * d].transpose(0, 2, 1)
    wao = w_attn_out.astype(BF16)
    wco = w_conv_out.astype(BF16)
    to_feature_major = lambda pool: pool.transpose(0, 1, 3, 4, 2).reshape(pool.shape[:2] + (d, pool.shape[2]))
    pool_kt = to_feature_major(cache_k)
    pool_vt = to_feature_major(cache_v)

    outs = {name: [] for name in ("kp", "vp", "cp", "ks", "vs", "cs")}
    for i in range(depth):
        g = norm_gain[i]
        ffn = lambda x, half, tm, mix=None: _ffn_half(x, g[4 * half:4 * half + 2], wup[i, half], wdn[i, half], tm, mix)
        xp = ffn(xp, 0, tm_p)
        xs = ffn(xs, 0, tm_s)
        g_pre, g_post = g[2:3], g[3:4]
        if i % n_mixers == 0:
            a = i // n_mixers
            bias = attn_logit_bias[a].astype(F32) * LOG2E
            n_s = dec_batch * t_new
            qp, kpf, vpf, kpb, vpb = _attn_pre(xp, g_pre, wqkv[a], wkt[a], tm_p, batch, seq)
            qs, ksf, vsf, ksb, vsb = _attn_pre(xs, g_pre, wqkv[a], wkt[a], tm_s, 1, n_s)
            op = _prompt_attention(qp, kpb, vpb, bias, batch, seq, tq)
            per_seq_t = lambda xt: xt.reshape(d, dec_batch, t_new).transpose(1, 0, 2)
            os_ = _sample_attention(qs, per_seq_t(ksb), vsb, bias, pool_kt, pool_vt, page_table, a,
                                    dec_batch, t_new, pps=min(PAGES_PER_STEP, page_table.shape[1]))
            mix_p = (op, g_post, wao[a])
            xs = _mix_post(os_, xs, g_post, wao[a], tm_s)
            heads_last = lambda xt, b, t: xt.reshape(b, n_heads, HEAD_DIM, t).transpose(0, 3, 1, 2)
            outs["kp"].append(heads_last(kpf, batch, seq)); outs["vp"].append(heads_last(vpf, batch, seq))
            outs["ks"].append(heads_last(per_seq_t(ksf), dec_batch, t_new))
            outs["vs"].append(heads_last(per_seq_t(vsf), dec_batch, t_new))
        else:
            c = i // n_mixers
            n_state = state_conv.shape[2]
            assert n_state == conv_w.shape[1] - 1 == 2 and seq >= n_state and t_new >= n_state and t_new == HALO_ROWS
            up_, bp, tails = _conv_pre(xp, g_pre, wcin[c], tm_p, narrow=True)
            us_, bs = _conv_pre(xs, g_pre, wcin[c], tm_s, narrow=False)
            xp = _conv_post(up_, tails.reshape(-1, d), bp, xp, conv_w[c], g_post, wco[c], tm_p, seq, zero_state=True)
            halo_s = jnp.pad(state_conv[c], ((0, 0), (HALO_ROWS - n_state, 0), (0, 0))).reshape(-1, d)
            xs = _conv_post(us_, halo_s, bs, xs, conv_w[c], g_post, wco[c], t_new, t_new, zero_state=False)
            last_tiles = tails.reshape(batch, seq // tm_p, HALO_ROWS, d)[:, -1]
            outs["cp"].append(last_tiles[:, HALO_ROWS - n_state:])
            outs["cs"].append(us_.reshape(dec_batch, t_new, d)[:, t_new - n_state:])
            mix_p = None
        xp = ffn(xp, 1, tm_p, mix_p)
        xs = ffn(xs, 1, tm_s)

    return (xp.reshape(batch, seq, d), xs.reshape(dec_batch, t_new, d),
            jnp.stack(outs["kp"]), jnp.stack(outs["vp"]), jnp.stack(outs["cp"]),
            jnp.stack(outs["ks"]), jnp.stack(outs["vs"]), jnp.stack(outs["cs"]))
```

```python
import functools

import jax
import jax.numpy as jnp
from jax import lax
from jax.experimental import pallas as pl
from jax.experimental.pallas import tpu as pltpu

F32 = jnp.float32
BF16 = jnp.bfloat16

RMS_EPS = 1e-6
HEAD_DIM = 64
LANES = 128
FF_CHUNK = 256
HALO_ROWS = 8
VMEM_LIMIT = 56 * 1024 * 1024


def _params(sem):
    return pltpu.CompilerParams(dimension_semantics=sem, vmem_limit_bytes=VMEM_LIMIT)


def _resident(shape):
    nd = len(shape)
    return pl.BlockSpec(shape, lambda *_: (0,) * nd, pipeline_mode=pl.Buffered(1))


def _rms(x, g):
    ms = jnp.mean(x * x, axis=-1, keepdims=True)
    return (x * lax.rsqrt(ms + RMS_EPS)) * g


LOG2E = 1.4426950408889634
EXP2_MAX = 126.0


def _softplus2(z):
    return jnp.maximum(z, jnp.log(1.0 + jnp.exp2(jnp.minimum(z, EXP2_MAX))) * LOG2E)


def _ffn_kernel(*refs, d_ff, mixed):
    if mixed:
        x_ref, g_ref, wup_ref, wdn_ref, om_ref, gm_ref, wm_ref, o_ref = refs
        x = x_ref[...] + _rms(jnp.dot(om_ref[...], wm_ref[...], preferred_element_type=F32), gm_ref[...])
    else:
        x_ref, g_ref, wup_ref, wdn_ref, o_ref = refs
        x = x_ref[...]
    h = _rms(x, g_ref[0:1, :]).astype(BF16)
    acc = jnp.zeros(x.shape, F32)
    for lo in range(0, d_ff, FF_CHUNK):
        hi = lo + FF_CHUNK
        gate = jnp.dot(h, wup_ref[:, lo:hi], preferred_element_type=F32)
        up = jnp.dot(h, wup_ref[:, d_ff + lo:d_ff + hi], preferred_element_type=F32)
        a = ((gate * jax.nn.sigmoid(gate)) * up).astype(BF16)
        acc = acc + jnp.dot(a, wdn_ref[lo:hi, :], preferred_element_type=F32)
    o_ref[...] = x + 0.5 * _rms(acc, g_ref[1:2, :])


def _ffn_half(x, g2, wup, wdn, tm, mix=None):
    n, d = x.shape
    row = pl.BlockSpec((tm, d), lambda i: (i, 0))
    operands, specs = [x, g2, wup, wdn], [row, _resident(g2.shape), _resident(wup.shape), _resident(wdn.shape)]
    if mix is not None:
        o, g_mix, w_mix = mix
        operands += [o, g_mix, w_mix]
        specs += [row, _resident(g_mix.shape), _resident(w_mix.shape)]
    return pl.pallas_call(
        functools.partial(_ffn_kernel, d_ff=wdn.shape[0], mixed=mix is not None),
        grid=(n // tm,),
        in_specs=specs,
        out_specs=row,
        out_shape=jax.ShapeDtypeStruct((n, d), F32),
        compiler_params=_params(("parallel",)),
        name="ffn_half",
    )(*operands)


def _attn_pre_kernel(x_ref, g_ref, w_ref, wkt_ref, q_ref, ktf_ref, vtf_ref, ktb_ref, vb_ref):
    d = x_ref.shape[1]
    h = _rms(x_ref[...], g_ref[...]).astype(BF16)
    q = jnp.dot(h, w_ref[:, :d], preferred_element_type=F32)
    q_ref[...] = (q * (HEAD_DIM ** -0.5 * LOG2E)).astype(BF16)
    kt = lax.dot_general(wkt_ref[...], h, (((1,), (1,)), ((), ())), preferred_element_type=F32)
    ktf_ref[...] = kt
    ktb_ref[...] = kt.astype(BF16)
    v = jnp.dot(h, w_ref[:, 2 * d:], preferred_element_type=F32)
    vtf_ref[...] = v.T
    vb_ref[...] = v.astype(BF16)


def _attn_pre(x, g, w, wkt, tm, batch, seq):
    n, d = x.shape
    tiles_per_seq = seq // tm
    assert n == batch * seq and seq % tm == 0
    row = pl.BlockSpec((tm, d), lambda i: (i, 0))
    col = pl.BlockSpec((None, d, tm), lambda i: (i // tiles_per_seq, 0, i % tiles_per_seq))
    feature_major = lambda dt: jax.ShapeDtypeStruct((batch, d, seq), dt)
    return pl.pallas_call(
        _attn_pre_kernel,
        grid=(n // tm,),
        in_specs=[row, _resident(g.shape), _resident(w.shape), _resident(wkt.shape)],
        out_specs=[row, col, col, col, row],
        out_shape=[jax.ShapeDtypeStruct((n, d), BF16), feature_major(F32), feature_major(F32),
                   feature_major(BF16), jax.ShapeDtypeStruct((n, d), BF16)],
        compiler_params=_params(("parallel",)),
        name="attn_pre",
    )(x, g, w, wkt)


def _mix_post_kernel(o_ref, x_ref, g_ref, w_ref, y_ref):
    m = jnp.dot(o_ref[...], w_ref[...], preferred_element_type=F32)
    y_ref[...] = x_ref[...] + _rms(m, g_ref[...])


def _mix_post(o, x, g, w, tm):
    n, d = x.shape
    row = pl.BlockSpec((tm, d), lambda i: (i, 0))
    return pl.pallas_call(
        _mix_post_kernel,
        grid=(n // tm,),
        in_specs=[row, row, _resident(g.shape), _resident(w.shape)],
        out_specs=row,
        out_shape=jax.ShapeDtypeStruct((n, d), F32),
        compiler_params=_params(("parallel",)),
        name="mix_post",
    )(o, x, g, w)


def _later_matrix(tk, sum_cols=0):
    j = lax.broadcasted_iota(jnp.int32, (tk, tk + sum_cols), 0)
    s = lax.broadcasted_iota(jnp.int32, (tk, tk + sum_cols), 1)
    return jnp.where((j > s) | (s >= tk), 1.0, 0.0).astype(BF16)


def _sb_logits_stage(z, causal):
    sp = _softplus2(z)
    lsig = z - sp
    if causal is not None:
        sp = jnp.where(causal, sp, 0.0)
        lsig = jnp.where(causal, lsig, -jnp.inf)
    return lsig, sp


def _sb_weights_stage(lsig, later, c):
    if c.shape[1] > 1:
        c = jnp.concatenate([c] * (lsig.shape[1] // c.shape[1]), axis=1)
    return jnp.exp2(lsig - later - c)


PAIRS_PER_STEP = 8


def _prompt_attn_kernel(bias_ref, q_ref, kt_ref, v_ref, o_ref, c_ref, acc_ref, *, tq):
    i = pl.program_id(2)
    lane = lax.broadcasted_iota(jnp.int32, (tq, LANES), 1)
    row = lax.broadcasted_iota(jnp.int32, (2 * tq, 1), 0)
    later_mat = _later_matrix(tq)
    lanes_of = lambda g: slice(g * LANES, (g + 1) * LANES)
    rows_of = lambda kt: pl.ds(pl.multiple_of(kt * tq, tq), tq)

    n_heads = bias_ref.shape[0] // 2
    lane2 = lax.broadcasted_iota(jnp.int32, (2 * tq, LANES), 1)
    ones_rows = jnp.where(lax.broadcasted_iota(jnp.int32, (LANES, tq), 0) < 2, 1.0, 0.0).astype(BF16)
    qs = []
    n_pairs = acc_ref.shape[0]
    for g in range(n_pairs):
        q = q_ref[:, lanes_of(g)]
        zero = jnp.zeros_like(q)
        q2 = jnp.concatenate([jnp.where(lane < HEAD_DIM, q, zero), jnp.where(lane >= HEAD_DIM, q, zero)], axis=0)
        head = 2 * (n_pairs * pl.program_id(1) + g)
        hi = jnp.where(row < tq, bias_ref[head], bias_ref[head + 1])
        lo = jnp.where(row < tq, bias_ref[n_heads + head], bias_ref[n_heads + head + 1])
        bias_cols = jnp.where(lane2 == 0, hi, jnp.where(lane2 == 1, lo, 0.0)).astype(BF16)
        qs.append(jnp.concatenate([q2, bias_cols], axis=1))

    def tiles(kt, causal):
        for g in range(n_pairs):
            keys = jnp.concatenate([kt_ref[lanes_of(g), rows_of(kt)], ones_rows], axis=0)
            lsig, sp = _sb_logits_stage(jnp.dot(qs[g], keys, preferred_element_type=F32), causal)
            later = jnp.dot(sp.astype(BF16), later_mat, preferred_element_type=F32)
            c = c_ref[g]
            w = _sb_weights_stage(lsig, later, c)
            c_ref[g] = c + jnp.sum(sp, axis=-1, keepdims=True)
            acc_ref[g] += jnp.dot(w.astype(BF16), v_ref[rows_of(kt), lanes_of(g)], preferred_element_type=F32)

    t = lax.broadcasted_iota(jnp.int32, (2 * tq, tq), 0)
    t = jnp.where(t >= tq, t - tq, t)
    s = lax.broadcasted_iota(jnp.int32, (2 * tq, tq), 1)
    c_ref[...] = jnp.zeros_like(c_ref)
    acc_ref[...] = jnp.zeros_like(acc_ref)
    tiles(i, s < t)
    odd = i % 2
    pl.when(odd == 1)(lambda: tiles(i - 1, None))

    @pl.loop(0, i // 2)
    def _(n):
        kt = i - 1 - odd - 2 * n
        tiles(kt, None)
        tiles(kt - 1, None)

    for g in range(n_pairs):
        acc = acc_ref[g]
        o_ref[:, lanes_of(g)] = jnp.where(lane < HEAD_DIM, acc[:tq], acc[tq:]).astype(o_ref.dtype)


def _prompt_attention(q, kt, v, bias, batch, seq, tq):
    n, d = q.shape
    n_pairs = min(PAIRS_PER_STEP, d // LANES)
    width = n_pairs * LANES
    assert d % width == 0
    nq = seq // tq
    q_spec = pl.BlockSpec((tq, width), lambda b, p, i: (b * nq + i, p))
    kt_spec = pl.BlockSpec((None, width, seq), lambda b, p, i: (b, p, 0), pipeline_mode=pl.Buffered(1))
    v_spec = pl.BlockSpec((seq, width), lambda b, p, i: (b, p), pipeline_mode=pl.Buffered(1))
    hi = bias.astype(BF16).astype(F32)
    bias = jnp.concatenate([hi, (bias - hi).astype(BF16).astype(F32)])
    per_pair = lambda cols, dt: pltpu.VMEM((n_pairs, 2 * tq, cols), dt)
    return pl.pallas_call(
        functools.partial(_prompt_attn_kernel, tq=tq),
        grid=(batch, d // width, nq),
        in_specs=[pl.BlockSpec(memory_space=pltpu.SMEM), q_spec, kt_spec, v_spec],
        out_specs=q_spec,
        out_shape=jax.ShapeDtypeStruct((n, d), BF16),
        scratch_shapes=[per_pair(LANES, F32), per_pair(LANES, F32)],
        compiler_params=_params(("parallel", "parallel", "arbitrary")),
        name="prompt_attn",
    )(bias, q, kt, v)


PAGES_PER_TILE = 2
PAGES_PER_STEP = 16


def _sample_attn_kernel(pt_ref, qbd_ref, bcol_ref, kn_ref, vn_ref, *rest, n_heads, t_new, pps):
    del pt_ref
    k_refs = rest[:pps]
    v_refs = rest[pps:2 * pps]
    o_ref, acc_ref, c_ref = rest[2 * pps:]
    j = pl.program_id(1)
    rows = n_heads * t_new
    page = kn_ref.shape[1]
    qbd = qbd_ref[...]
    bcol = bcol_ref[...]

    def tiles(kt, vt, tk, causal):
        n = kt.shape[1] // tk
        z = jnp.dot(qbd, kt, preferred_element_type=F32) + bcol
        lsig, sp = _sb_logits_stage(z, causal)
        stacked = jnp.concatenate([sp[:, a * tk:(a + 1) * tk] for a in range(n)], axis=0).astype(BF16)
        sums = jnp.dot(stacked, _later_matrix(tk, LANES), preferred_element_type=F32)
        of_tile = lambda a: sums[a * rows:(a + 1) * rows]
        later = jnp.concatenate([of_tile(a)[:, :tk] for a in range(n)], axis=1)
        c, carries = c_ref[...], [None] * n
        for a in reversed(range(n)):
            carries[a] = c
            c = c + of_tile(a)[:, tk:]
        c_ref[...] = c
        w = _sb_weights_stage(lsig, later, jnp.concatenate([x for x in carries for _ in range(tk // LANES)], axis=1))
        acc_ref[...] += lax.dot_general(w.astype(BF16), vt, (((1,), (1,)), ((), ())), preferred_element_type=F32)

    @pl.when(j == 0)
    def _():
        acc_ref[...] = jnp.zeros_like(acc_ref)
        c_ref[...] = jnp.zeros_like(c_ref)
        t = lax.broadcasted_iota(jnp.int32, (rows, page), 0) % t_new
        s = lax.broadcasted_iota(jnp.int32, (rows, page), 1)
        tiles(kn_ref[...], vn_ref[...], page, s < t)

    ascending = range(pps - 1, -1, -1)
    kt = jnp.concatenate([k_refs[r][...].astype(BF16) for r in ascending], axis=1)
    vt = jnp.concatenate([v_refs[r][...].astype(BF16) for r in ascending], axis=1)
    tiles(kt, vt, PAGES_PER_TILE * page, None)

    @pl.when(j == pl.num_programs(1) - 1)
    def _():
        d = acc_ref.shape[1]
        head_of_lane = lax.broadcasted_iota(jnp.int32, (t_new, d), 1) // HEAD_DIM
        o = jnp.zeros((t_new, d), F32)
        for h in range(n_heads):
            o = jnp.where(head_of_lane == h, acc_ref[h * t_new:(h + 1) * t_new, :], o)
        o_ref[...] = o.astype(o_ref.dtype)


def _sample_attention(q, kt_new, v_new, bias, kt_pool, vt_pool, page_table, layer, dec_batch, t_new, pps):
    n, d = q.shape
    n_heads = d // HEAD_DIM
    rows = n_heads * t_new
    page = kt_pool.shape[3]
    n_pages = page_table.shape[1]
    assert n_pages % pps == 0 and pps % PAGES_PER_TILE == 0 and t_new <= page
    q4 = q.reshape(dec_batch, t_new, n_heads, HEAD_DIM)
    eye = jnp.eye(n_heads, dtype=q.dtype)
    qbd = jnp.einsum("nthd,hg->nhtgd", q4, eye).reshape(dec_batch, rows, d)
    bcol = jnp.repeat(bias.astype(F32), t_new).reshape(rows, 1)
    pad = ((0, 0), (0, 0), (0, page - t_new))
    kn = jnp.pad(kt_new, pad)
    vn = jnp.pad(v_new.reshape(dec_batch, t_new, d).transpose(0, 2, 1), pad)

    per_seq = lambda shape: pl.BlockSpec((None,) + shape, lambda b, j, pt: (b, 0, 0))

    def page_spec(r):
        return pl.BlockSpec((None, None, d, page),
                            lambda b, j, pt: (layer, pt[b, n_pages - 1 - (j * pps + r)], 0, 0))

    grid_spec = pltpu.PrefetchScalarGridSpec(
        num_scalar_prefetch=1,
        grid=(dec_batch, n_pages // pps),
        in_specs=[per_seq((rows, d)), pl.BlockSpec((rows, 1), lambda b, j, pt: (0, 0)),
                  per_seq((d, page)), per_seq((d, page))]
        + [page_spec(r) for r in range(pps)] * 2,
        out_specs=per_seq((t_new, d)),
        scratch_shapes=[pltpu.VMEM((rows, d), F32), pltpu.VMEM((rows, LANES), F32)],
    )
    o = pl.pallas_call(
        functools.partial(_sample_attn_kernel, n_heads=n_heads, t_new=t_new, pps=pps),
        grid_spec=grid_spec,
        out_shape=jax.ShapeDtypeStruct((dec_batch, t_new, d), BF16),
        compiler_params=_params(("parallel", "arbitrary")),
        name="sample_attn",
    )(page_table, qbd, bcol, kn, vn, *([kt_pool] * pps), *([vt_pool] * pps))
    return o.reshape(n, d)


def _conv_pre_kernel(x_ref, g_ref, w_ref, u_ref, b_ref, *tail_ref):
    d = x_ref.shape[1]
    h = _rms(x_ref[...], g_ref[...]).astype(BF16)
    b_ref[...] = jnp.dot(h, w_ref[:, :d], preferred_element_type=F32).astype(b_ref.dtype)
    c_gate = jnp.dot(h, w_ref[:, d:2 * d], preferred_element_type=F32)
    val = jnp.dot(h, w_ref[:, 2 * d:], preferred_element_type=F32)
    u = c_gate * val
    u_ref[...] = u.astype(u_ref.dtype)
    if tail_ref:
        tail_ref[0][...] = u[u.shape[0] - HALO_ROWS:]


def _conv_pre(x, g, w, tm, narrow):
    n, d = x.shape
    row = pl.BlockSpec((tm, d), lambda i: (i, 0))
    dt = BF16 if narrow else F32
    out_specs, out_shape = [row, row], [jax.ShapeDtypeStruct((n, d), dt)] * 2
    if narrow:
        out_specs.append(pl.BlockSpec((None, HALO_ROWS, d), lambda i: (i, 0, 0)))
        out_shape.append(jax.ShapeDtypeStruct((n // tm, HALO_ROWS, d), F32))
    return pl.pallas_call(
        _conv_pre_kernel,
        grid=(n // tm,),
        in_specs=[row, _resident(g.shape), _resident(w.shape)],
        out_specs=out_specs,
        out_shape=out_shape,
        compiler_params=_params(("parallel",)),
        name="conv_pre",
    )(x, g, w)


def _conv_post_kernel(u_ref, halo_ref, b_ref, x_ref, cw_ref, g_ref, w_ref, y_ref, *, tiles_per_seq, zero_state):
    u = u_ref[...].astype(F32)
    prev1 = halo_ref[HALO_ROWS - 1:HALO_ROWS, :]
    prev2 = halo_ref[HALO_ROWS - 2:HALO_ROWS - 1, :]
    if zero_state:
        first = (pl.program_id(0) % tiles_per_seq) == 0
        prev1 = jnp.where(first, 0.0, prev1)
        prev2 = jnp.where(first, 0.0, prev2)
    row = lax.broadcasted_iota(jnp.int32, u.shape, 0)
    u1 = jnp.where(row == 0, prev1, pltpu.roll(u, 1, 0))
    u2 = jnp.where(row == 0, prev2, jnp.where(row == 1, prev1, pltpu.roll(u, 2, 0)))
    conv = cw_ref[0:1, :] * u2 + cw_ref[1:2, :] * u1 + cw_ref[2:3, :] * u
    y = (b_ref[...].astype(F32) * conv).astype(BF16)
    m = jnp.dot(y, w_ref[...], preferred_element_type=F32)
    y_ref[...] = x_ref[...] + _rms(m, g_ref[...])


def _conv_post(u, halo_src, b, x, conv_w, g, w, tm, seq, zero_state):
    n, d = x.shape
    assert seq % tm == 0 and tm % HALO_ROWS == 0
    row = pl.BlockSpec((tm, d), lambda i: (i, 0))
    if zero_state:
        halo = pl.BlockSpec((HALO_ROWS, d), lambda i: (jnp.maximum(i - 1, 0), 0))
    else:
        halo = pl.BlockSpec((HALO_ROWS, d), lambda i: (i, 0))
    return pl.pallas_call(
        functools.partial(_conv_post_kernel, tiles_per_seq=seq // tm, zero_state=zero_state),
        grid=(n // tm,),
        in_specs=[row, halo, row, row, _resident(conv_w.shape), _resident(g.shape), _resident(w.shape)],
        out_specs=row,
        out_shape=jax.ShapeDtypeStruct((n, d), F32),
        compiler_params=_params(("parallel",)),
        name="conv_post",
    )(u, halo_src, b, x, conv_w, g, w)


def _row_tile(n, target):
    tm = min(n, target)
    assert n % tm == 0
    return tm


def kernel(x_prompt, x_sample, cache_k, cache_v, state_conv, page_table, norm_gain,
           w_ffn_up, w_ffn_down, w_attn_qkv, w_attn_out, attn_logit_bias, w_conv_in, conv_w, w_conv_out):
    batch, seq, d = x_prompt.shape
    dec_batch, t_new, _ = x_sample.shape
    depth = norm_gain.shape[0]
    d_ff = w_ffn_down.shape[2]
    n_heads = attn_logit_bias.shape[1]
    assert d == n_heads * HEAD_DIM and d % LANES == 0 and d_ff % FF_CHUNK == 0
    n_mixers = 2

    xp = x_prompt.reshape(batch * seq, d)
    xs = x_sample.reshape(dec_batch * t_new, d)
    tm_p = _row_tile(batch * seq, 512)
    tm_s = _row_tile(dec_batch * t_new, 256)
    tq = _row_tile(seq, 256)
    assert seq % tm_p == 0

    wup, wdn = w_ffn_up.astype(BF16), w_ffn_down.astype(BF16)
    wqkv, wcin = w_attn_qkv.astype(BF16), w_conv_in.astype(BF16)
    wkt = wqkv[:, :, d:2 * d].transpose(0, 2, 1)
    wao = w_attn_out.astype(BF16)
    wco = w_conv_out.astype(BF16)
    to_feature_major = lambda pool: pool.transpose(0, 1, 3, 4, 2).reshape(pool.shape[:2] + (d, pool.shape[2]))
    pool_kt = to_feature_major(cache_k)
    pool_vt = to_feature_major(cache_v)

    outs = {name: [] for name in ("kp", "vp", "cp", "ks", "vs", "cs")}
    for i in range(depth):
        g = norm_gain[i]
        ffn = lambda x, half, tm, mix=None: _ffn_half(x, g[4 * half:4 * half + 2], wup[i, half], wdn[i, half], tm, mix)
        xp = ffn(xp, 0, tm_p)
        xs = ffn(xs, 0, tm_s)
        g_pre, g_post = g[2:3], g[3:4]
        if i % n_mixers == 0:
            a = i // n_mixers
            bias = attn_logit_bias[a].astype(F32) * LOG2E
            n_s = dec_batch * t_new
            qp, kpf, vpf, kpb, vpb = _attn_pre(xp, g_pre, wqkv[a], wkt[a], tm_p, batch, seq)
            qs, ksf, vsf, ksb, vsb = _attn_pre(xs, g_pre, wqkv[a], wkt[a], tm_s, 1, n_s)
            op = _prompt_attention(qp, kpb, vpb, bias, batch, seq, tq)
            per_seq_t = lambda xt: xt.reshape(d, dec_batch, t_new).transpose(1, 0, 2)
            os_ = _sample_attention(qs, per_seq_t(ksb), vsb, bias, pool_kt, pool_vt, page_table, a,
                                    dec_batch, t_new, pps=min(PAGES_PER_STEP, page_table.shape[1]))
            mix_p = (op, g_post, wao[a])
            xs = _mix_post(os_, xs, g_post, wao[a], tm_s)
            heads_last = lambda xt, b, t: xt.reshape(b, n_heads, HEAD_DIM, t).transpose(0, 3, 1, 2)
            outs["kp"].append(heads_last(kpf, batch, seq)); outs["vp"].append(heads_last(vpf, batch, seq))
            outs["ks"].append(heads_last(per_seq_t(ksf), dec_batch, t_new))
            outs["vs"].append(heads_last(per_seq_t(vsf), dec_batch, t_new))
        else:
            c = i // n_mixers
            n_state = state_conv.shape[2]
            assert n_state == conv_w.shape[1] - 1 == 2 and seq >= n_state and t_new >= n_state and t_new == HALO_ROWS
            up_, bp, tails = _conv_pre(xp, g_pre, wcin[c], tm_p, narrow=True)
            us_, bs = _conv_pre(xs, g_pre, wcin[c], tm_s, narrow=False)
            xp = _conv_post(up_, tails.reshape(-1, d), bp, xp, conv_w[c], g_post, wco[c], tm_p, seq, zero_state=True)
            halo_s = jnp.pad(state_conv[c], ((0, 0), (HALO_ROWS - n_state, 0), (0, 0))).reshape(-1, d)
            xs = _conv_post(us_, halo_s, bs, xs, conv_w[c], g_post, wco[c], t_new, t_new, zero_state=False)
            last_tiles = tails.reshape(batch, seq // tm_p, HALO_ROWS, d)[:, -1]
            outs["cp"].append(last_tiles[:, HALO_ROWS - n_state:])
            outs["cs"].append(us_.reshape(dec_batch, t_new, d)[:, t_new - n_state:])
            mix_p = None
        xp = ffn(xp, 1, tm_p, mix_p)
        xs = ffn(xs, 1, tm_s)

    return (xp.reshape(batch, seq, d), xs.reshape(dec_batch, t_new, d),
            jnp.stack(outs["kp"]), jnp.stack(outs["vp"]), jnp.stack(outs["cp"]),
            jnp.stack(outs["ks"]), jnp.stack(outs["vs"]), jnp.stack(outs["cs"]))
```

```python
import functools

import jax
import jax.numpy as jnp
from jax import lax
from jax.experimental import pallas as pl
from jax.experimental.pallas import tpu as pltpu

F32 = jnp.float32
BF16 = jnp.bfloat16

RMS_EPS = 1e-6
HEAD_DIM = 64
LANES = 128
FF_CHUNK = 256
HALO_ROWS = 8
VMEM_LIMIT = 56 * 1024 * 1024


def _params(sem):
    return pltpu.CompilerParams(dimension_semantics=sem, vmem_limit_bytes=VMEM_LIMIT)


def _resident(shape):
    nd = len(shape)
    return pl.BlockSpec(shape, lambda *_: (0,) * nd, pipeline_mode=pl.Buffered(1))


def _rms(x, g):
    ms = jnp.mean(x * x, axis=-1, keepdims=True)
    return (x * lax.rsqrt(ms + RMS_EPS)) * g


LOG2E = 1.4426950408889634
EXP2_MAX = 126.0


def _softplus2(z):
    return jnp.maximum(z, jnp.log(1.0 + jnp.exp2(jnp.minimum(z, EXP2_MAX))) * LOG2E)


def _swiglu_half_step(x, g_ref, wup_ref, wdn_ref, d_ff):
    h = _rms(x, g_ref[0:1, :]).astype(BF16)
    acc = jnp.zeros(x.shape, F32)
    for lo in range(0, d_ff, FF_CHUNK):
        hi = lo + FF_CHUNK
        gate = jnp.dot(h, wup_ref[:, lo:hi], preferred_element_type=F32)
        up = jnp.dot(h, wup_ref[:, d_ff + lo:d_ff + hi], preferred_element_type=F32)
        a = ((gate * jax.nn.sigmoid(gate)) * up).astype(BF16)
        acc = acc + jnp.dot(a, wdn_ref[lo:hi, :], preferred_element_type=F32)
    return x + 0.5 * _rms(acc, g_ref[1:2, :])


def _ffn_kernel(*refs, d_ff, mixed):
    if mixed:
        x_ref, g_ref, wup_ref, wdn_ref, om_ref, gm_ref, wm_ref, o_ref = refs
        x = x_ref[...] + _rms(jnp.dot(om_ref[...], wm_ref[...], preferred_element_type=F32), gm_ref[...])
    else:
        x_ref, g_ref, wup_ref, wdn_ref, o_ref = refs
        x = x_ref[...]
    o_ref[...] = _swiglu_half_step(x, g_ref, wup_ref, wdn_ref, d_ff)


def _ffn_half(x, g2, wup, wdn, tm, mix=None):
    n, d = x.shape
    row = pl.BlockSpec((tm, d), lambda i: (i, 0))
    operands, specs = [x, g2, wup, wdn], [row, _resident(g2.shape), _resident(wup.shape), _resident(wdn.shape)]
    if mix is not None:
        o, g_mix, w_mix = mix
        operands += [o, g_mix, w_mix]
        specs += [row, _resident(g_mix.shape), _resident(w_mix.shape)]
    return pl.pallas_call(
        functools.partial(_ffn_kernel, d_ff=wdn.shape[0], mixed=mix is not None),
        grid=(n // tm,),
        in_specs=specs,
        out_specs=row,
        out_shape=jax.ShapeDtypeStruct((n, d), F32),
        compiler_params=_params(("parallel",)),
        name="ffn_half",
    )(*operands)


def _attn_pre_kernel(x_ref, g_ref, w_ref, wkt_ref, q_ref, ktf_ref, vtf_ref, ktb_ref, vb_ref):
    d = x_ref.shape[1]
    h = _rms(x_ref[...], g_ref[...]).astype(BF16)
    q = jnp.dot(h, w_ref[:, :d], preferred_element_type=F32)
    q_ref[...] = (q * (HEAD_DIM ** -0.5 * LOG2E)).astype(BF16)
    kt = lax.dot_general(wkt_ref[...], h, (((1,), (1,)), ((), ())), preferred_element_type=F32)
    ktf_ref[...] = kt
    ktb_ref[...] = kt.astype(BF16)
    v = jnp.dot(h, w_ref[:, 2 * d:], preferred_element_type=F32)
    vtf_ref[...] = v.T
    vb_ref[...] = v.astype(BF16)


def _attn_pre(x, g, w, wkt, tm, batch, seq):
    n, d = x.shape
    tiles_per_seq = seq // tm
    assert n == batch * seq and seq % tm == 0
    row = pl.BlockSpec((tm, d), lambda i: (i, 0))
    col = pl.BlockSpec((None, d, tm), lambda i: (i // tiles_per_seq, 0, i % tiles_per_seq))
    feature_major = lambda dt: jax.ShapeDtypeStruct((batch, d, seq), dt)
    return pl.pallas_call(
        _attn_pre_kernel,
        grid=(n // tm,),
        in_specs=[row, _resident(g.shape), _resident(w.shape), _resident(wkt.shape)],
        out_specs=[row, col, col, col, row],
        out_shape=[jax.ShapeDtypeStruct((n, d), BF16), feature_major(F32), feature_major(F32),
                   feature_major(BF16), jax.ShapeDtypeStruct((n, d), BF16)],
        compiler_params=_params(("parallel",)),
        name="attn_pre",
    )(x, g, w, wkt)


def _mix_post_kernel(o_ref, x_ref, g_ref, w_ref, y_ref):
    m = jnp.dot(o_ref[...], w_ref[...], preferred_element_type=F32)
    y_ref[...] = x_ref[...] + _rms(m, g_ref[...])


def _mix_post(o, x, g, w, tm):
    n, d = x.shape
    row = pl.BlockSpec((tm, d), lambda i: (i, 0))
    return pl.pallas_call(
        _mix_post_kernel,
        grid=(n // tm,),
        in_specs=[row, row, _resident(g.shape), _resident(w.shape)],
        out_specs=row,
        out_shape=jax.ShapeDtypeStruct((n, d), F32),
        compiler_params=_params(("parallel",)),
        name="mix_post",
    )(o, x, g, w)


def _later_matrix(tk, sum_cols=0):
    j = lax.broadcasted_iota(jnp.int32, (tk, tk + sum_cols), 0)
    s = lax.broadcasted_iota(jnp.int32, (tk, tk + sum_cols), 1)
    return jnp.where((j > s) | (s >= tk), 1.0, 0.0).astype(BF16)


def _sb_logits_stage(z, causal):
    sp = _softplus2(z)
    lsig = z - sp
    if causal is not None:
        sp = jnp.where(causal, sp, 0.0)
        lsig = jnp.where(causal, lsig, -jnp.inf)
    return lsig, sp


def _sb_weights_stage(lsig, later, c):
    if c.shape[1] > 1:
        c = jnp.concatenate([c] * (lsig.shape[1] // c.shape[1]), axis=1)
    return jnp.exp2(lsig - later - c)


PAIRS_PER_STEP = 8


def _prompt_attn_kernel(bias_ref, q_ref, kt_ref, v_ref, o_ref, c_ref, acc_ref, *, tq):
    i = pl.program_id(2)
    lane = lax.broadcasted_iota(jnp.int32, (tq, LANES), 1)
    row = lax.broadcasted_iota(jnp.int32, (2 * tq, 1), 0)
    later_mat = _later_matrix(tq)
    lanes_of = lambda g: slice(g * LANES, (g + 1) * LANES)
    rows_of = lambda kt: pl.ds(pl.multiple_of(kt * tq, tq), tq)

    n_heads = bias_ref.shape[0] // 2
    lane2 = lax.broadcasted_iota(jnp.int32, (2 * tq, LANES), 1)
    ones_rows = jnp.where(lax.broadcasted_iota(jnp.int32, (LANES, tq), 0) < 2, 1.0, 0.0).astype(BF16)
    qs = []
    n_pairs = acc_ref.shape[0]
    for g in range(n_pairs):
        q = q_ref[:, lanes_of(g)]
        zero = jnp.zeros_like(q)
        q2 = jnp.concatenate([jnp.where(lane < HEAD_DIM, q, zero), jnp.where(lane >= HEAD_DIM, q, zero)], axis=0)
        head = 2 * (n_pairs * pl.program_id(1) + g)
        hi = jnp.where(row < tq, bias_ref[head], bias_ref[head + 1])
        lo = jnp.where(row < tq, bias_ref[n_heads + head], bias_ref[n_heads + head + 1])
        bias_cols = jnp.where(lane2 == 0, hi, jnp.where(lane2 == 1, lo, 0.0)).astype(BF16)
        qs.append(jnp.concatenate([q2, bias_cols], axis=1))

    def tiles(kt, causal):
        for g in range(n_pairs):
            keys = jnp.concatenate([kt_ref[lanes_of(g), rows_of(kt)], ones_rows], axis=0)
            lsig, sp = _sb_logits_stage(jnp.dot(qs[g], keys, preferred_element_type=F32), causal)
            later = jnp.dot(sp.astype(BF16), later_mat, preferred_element_type=F32)
            c = c_ref[g]
            w = _sb_weights_stage(lsig, later, c)
            c_ref[g] = c + jnp.sum(sp, axis=-1, keepdims=True)
            acc_ref[g] += jnp.dot(w.astype(BF16), v_ref[rows_of(kt), lanes_of(g)], preferred_element_type=F32)

    t = lax.broadcasted_iota(jnp.int32, (2 * tq, tq), 0)
    t = jnp.where(t >= tq, t - tq, t)
    s = lax.broadcasted_iota(jnp.int32, (2 * tq, tq), 1)
    c_ref[...] = jnp.zeros_like(c_ref)
    acc_ref[...] = jnp.zeros_like(acc_ref)
    tiles(i, s < t)
    odd = i % 2
    pl.when(odd == 1)(lambda: tiles(i - 1, None))

    @pl.loop(0, i // 2)
    def _(n):
        kt = i - 1 - odd - 2 * n
        tiles(kt, None)
        tiles(kt - 1, None)

    for g in range(n_pairs):
        acc = acc_ref[g]
        o_ref[:, lanes_of(g)] = jnp.where(lane < HEAD_DIM, acc[:tq], acc[tq:]).astype(o_ref.dtype)


def _prompt_attention(q, kt, v, bias, batch, seq, tq):
    n, d = q.shape
    n_pairs = min(PAIRS_PER_STEP, d // LANES)
    width = n_pairs * LANES
    assert d % width == 0
    nq = seq // tq
    q_spec = pl.BlockSpec((tq, width), lambda b, p, i: (b * nq + i, p))
    kt_spec = pl.BlockSpec((None, width, seq), lambda b, p, i: (b, p, 0), pipeline_mode=pl.Buffered(1))
    v_spec = pl.BlockSpec((seq, width), lambda b, p, i: (b, p), pipeline_mode=pl.Buffered(1))
    hi = bias.astype(BF16).astype(F32)
    bias = jnp.concatenate([hi, (bias - hi).astype(BF16).astype(F32)])
    per_pair = lambda cols, dt: pltpu.VMEM((n_pairs, 2 * tq, cols), dt)
    return pl.pallas_call(
        functools.partial(_prompt_attn_kernel, tq=tq),
        grid=(batch, d // width, nq),
        in_specs=[pl.BlockSpec(memory_space=pltpu.SMEM), q_spec, kt_spec, v_spec],
        out_specs=q_spec,
        out_shape=jax.ShapeDtypeStruct((n, d), BF16),
        scratch_shapes=[per_pair(LANES, F32), per_pair(LANES, F32)],
        compiler_params=_params(("parallel", "parallel", "arbitrary")),
        name="prompt_attn",
    )(bias, q, kt, v)


PAGES_PER_TILE = 2
PAGES_PER_STEP = 16


def _sample_attn_kernel(pt_ref, qbd_ref, bcol_ref, kn_ref, vn_ref, *rest, n_heads, t_new, pps):
    del pt_ref
    k_refs = rest[:pps]
    v_refs = rest[pps:2 * pps]
    o_ref, acc_ref, c_ref = rest[2 * pps:]
    j = pl.program_id(1)
    rows = n_heads * t_new
    page = kn_ref.shape[1]
    qbd = qbd_ref[...]
    bcol = bcol_ref[...]

    def tiles(kt, vt, tk, causal):
        n = kt.shape[1] // tk
        z = jnp.dot(qbd, kt, preferred_element_type=F32) + bcol
        lsig, sp = _sb_logits_stage(z, causal)
        stacked = jnp.concatenate([sp[:, a * tk:(a + 1) * tk] for a in range(n)], axis=0).astype(BF16)
        sums = jnp.dot(stacked, _later_matrix(tk, LANES), preferred_element_type=F32)
        of_tile = lambda a: sums[a * rows:(a + 1) * rows]
        later = jnp.concatenate([of_tile(a)[:, :tk] for a in range(n)], axis=1)
        c, carries = c_ref[...], [None] * n
        for a in reversed(range(n)):
            carries[a] = c
            c = c + of_tile(a)[:, tk:]
        c_ref[...] = c
        w = _sb_weights_stage(lsig, later, jnp.concatenate([x for x in carries for _ in range(tk // LANES)], axis=1))
        acc_ref[...] += lax.dot_general(w.astype(BF16), vt, (((1,), (1,)), ((), ())), preferred_element_type=F32)

    @pl.when(j == 0)
    def _():
        acc_ref[...] = jnp.zeros_like(acc_ref)
        c_ref[...] = jnp.zeros_like(c_ref)
        t = lax.broadcasted_iota(jnp.int32, (rows, page), 0) % t_new
        s = lax.broadcasted_iota(jnp.int32, (rows, page), 1)
        tiles(kn_ref[...], vn_ref[...], page, s < t)

    ascending = range(pps - 1, -1, -1)
    kt = jnp.concatenate([k_refs[r][...].astype(BF16) for r in ascending], axis=1)
    vt = jnp.concatenate([v_refs[r][...].astype(BF16) for r in ascending], axis=1)
    tiles(kt, vt, PAGES_PER_TILE * page, None)

    @pl.when(j == pl.num_programs(1) - 1)
    def _():
        d = acc_ref.shape[1]
        head_of_lane = lax.broadcasted_iota(jnp.int32, (t_new, d), 1) // HEAD_DIM
        o = jnp.zeros((t_new, d), F32)
        for h in range(n_heads):
            o = jnp.where(head_of_lane == h, acc_ref[h * t_new:(h + 1) * t_new, :], o)
        o_ref[...] = o.astype(o_ref.dtype)


def _sample_attention(q, kt_new, v_new, bias, kt_pool, vt_pool, page_table, layer, dec_batch, t_new, pps):
    n, d = q.shape
    n_heads = d // HEAD_DIM
    rows = n_heads * t_new
    page = kt_pool.shape[3]
    n_pages = page_table.shape[1]
    assert n_pages % pps == 0 and pps % PAGES_PER_TILE == 0 and t_new <= page
    q4 = q.reshape(dec_batch, t_new, n_heads, HEAD_DIM)
    eye = jnp.eye(n_heads, dtype=q.dtype)
    qbd = jnp.einsum("nthd,hg->nhtgd", q4, eye).reshape(dec_batch, rows, d)
    bcol = jnp.repeat(bias.astype(F32), t_new).reshape(rows, 1)
    pad = ((0, 0), (0, 0), (0, page - t_new))
    kn = jnp.pad(kt_new, pad)
    vn = jnp.pad(v_new.reshape(dec_batch, t_new, d).transpose(0, 2, 1), pad)

    per_seq = lambda shape: pl.BlockSpec((None,) + shape, lambda b, j, pt: (b, 0, 0))

    def page_spec(r):
        return pl.BlockSpec((None, None, d, page),
                            lambda b, j, pt: (layer, pt[b, n_pages - 1 - (j * pps + r)], 0, 0))

    grid_spec = pltpu.PrefetchScalarGridSpec(
        num_scalar_prefetch=1,
        grid=(dec_batch, n_pages // pps),
        in_specs=[per_seq((rows, d)), pl.BlockSpec((rows, 1), lambda b, j, pt: (0, 0)),
                  per_seq((d, page)), per_seq((d, page))]
        + [page_spec(r) for r in range(pps)] * 2,
        out_specs=per_seq((t_new, d)),
        scratch_shapes=[pltpu.VMEM((rows, d), F32), pltpu.VMEM((rows, LANES), F32)],
    )
    o = pl.pallas_call(
        functools.partial(_sample_attn_kernel, n_heads=n_heads, t_new=t_new, pps=pps),
        grid_spec=grid_spec,
        out_shape=jax.ShapeDtypeStruct((dec_batch, t_new, d), BF16),
        compiler_params=_params(("parallel", "arbitrary")),
        name="sample_attn",
    )(page_table, qbd, bcol, kn, vn, *([kt_pool] * pps), *([vt_pool] * pps))
    return o.reshape(n, d)


def _conv_pre_kernel(x_ref, g_ref, w_ref, u_ref, b_ref):
    d = x_ref.shape[1]
    h = _rms(x_ref[...], g_ref[...]).astype(BF16)
    b_ref[...] = jnp.dot(h, w_ref[:, :d], preferred_element_type=F32)
    c_gate = jnp.dot(h, w_ref[:, d:2 * d], preferred_element_type=F32)
    val = jnp.dot(h, w_ref[:, 2 * d:], preferred_element_type=F32)
    u_ref[...] = c_gate * val


def _conv_pre(x, g, w, tm):
    n, d = x.shape
    row = pl.BlockSpec((tm, d), lambda i: (i, 0))
    return pl.pallas_call(
        _conv_pre_kernel,
        grid=(n // tm,),
        in_specs=[row, _resident(g.shape), _resident(w.shape)],
        out_specs=[row, row],
        out_shape=[jax.ShapeDtypeStruct((n, d), F32)] * 2,
        compiler_params=_params(("parallel",)),
        name="conv_pre",
    )(x, g, w)


def _gated_conv(u, b, prev1, prev2, cw_ref):
    row = lax.broadcasted_iota(jnp.int32, u.shape, 0)
    u1 = jnp.where(row == 0, prev1, pltpu.roll(u, 1, 0))
    u2 = jnp.where(row == 0, prev2, jnp.where(row == 1, prev1, pltpu.roll(u, 2, 0)))
    conv = cw_ref[0:1, :] * u2 + cw_ref[1:2, :] * u1 + cw_ref[2:3, :] * u
    return (b * conv).astype(BF16)


def _conv_post_kernel(u_ref, halo_ref, b_ref, x_ref, cw_ref, g_ref, w_ref, y_ref):
    prev1 = halo_ref[HALO_ROWS - 1:HALO_ROWS, :]
    prev2 = halo_ref[HALO_ROWS - 2:HALO_ROWS - 1, :]
    y = _gated_conv(u_ref[...], b_ref[...], prev1, prev2, cw_ref)
    y_ref[...] = x_ref[...] + _rms(jnp.dot(y, w_ref[...], preferred_element_type=F32), g_ref[...])


def _conv_ffn_kernel(x_ref, g_ref, wup_ref, wdn_ref, gpre_ref, win_ref, cw_ref, gpost_ref, wout_ref,
                     o_ref, tail_ref, carry_ref, *, d_ff, tiles_per_seq):
    x = x_ref[...]
    d = x.shape[1]
    h = _rms(x, gpre_ref[...]).astype(BF16)
    b = jnp.dot(h, win_ref[:, :d], preferred_element_type=F32)
    u = (jnp.dot(h, win_ref[:, d:2 * d], preferred_element_type=F32)
         * jnp.dot(h, win_ref[:, 2 * d:], preferred_element_type=F32))
    first = (pl.program_id(0) % tiles_per_seq) == 0
    prev1 = jnp.where(first, 0.0, carry_ref[HALO_ROWS - 1:HALO_ROWS, :])
    prev2 = jnp.where(first, 0.0, carry_ref[HALO_ROWS - 2:HALO_ROWS - 1, :])
    y = _gated_conv(u, b, prev1, prev2, cw_ref)
    tail = u[u.shape[0] - HALO_ROWS:]
    carry_ref[...] = tail
    tail_ref[...] = tail
    x = x + _rms(jnp.dot(y, wout_ref[...], preferred_element_type=F32), gpost_ref[...])
    o_ref[...] = _swiglu_half_step(x, g_ref, wup_ref, wdn_ref, d_ff)


def _conv_ffn_half(x, g2, wup, wdn, g_pre, w_in, conv_w, g_post, w_out, tm, seq):
    n, d = x.shape
    assert seq % tm == 0 and tm >= HALO_ROWS
    row = pl.BlockSpec((tm, d), lambda i: (i, 0))
    weights = [g2, wup, wdn, g_pre, w_in, conv_w, g_post, w_out]
    return pl.pallas_call(
        functools.partial(_conv_ffn_kernel, d_ff=wdn.shape[0], tiles_per_seq=seq // tm),
        grid=(n // tm,),
        in_specs=[row] + [_resident(w.shape) for w in weights],
        out_specs=[row, pl.BlockSpec((None, HALO_ROWS, d), lambda i: (i, 0, 0))],
        out_shape=[jax.ShapeDtypeStruct((n, d), F32), jax.ShapeDtypeStruct((n // tm, HALO_ROWS, d), F32)],
        scratch_shapes=[pltpu.VMEM((HALO_ROWS, d), F32)],
        compiler_params=_params(("arbitrary",)),
        name="conv_ffn_half",
    )(x, *weights)


def _conv_post(u, halo_src, b, x, conv_w, g, w, tm):
    n, d = x.shape
    assert tm % HALO_ROWS == 0
    row = pl.BlockSpec((tm, d), lambda i: (i, 0))
    halo = pl.BlockSpec((HALO_ROWS, d), lambda i: (i, 0))
    return pl.pallas_call(
        _conv_post_kernel,
        grid=(n // tm,),
        in_specs=[row, halo, row, row, _resident(conv_w.shape), _resident(g.shape), _resident(w.shape)],
        out_specs=row,
        out_shape=jax.ShapeDtypeStruct((n, d), F32),
        compiler_params=_params(("parallel",)),
        name="conv_post",
    )(u, halo_src, b, x, conv_w, g, w)


def _row_tile(n, target):
    tm = min(n, target)
    assert n % tm == 0
    return tm


def kernel(x_prompt, x_sample, cache_k, cache_v, state_conv, page_table, norm_gain,
           w_ffn_up, w_ffn_down, w_attn_qkv, w_attn_out, attn_logit_bias, w_conv_in, conv_w, w_conv_out):
    batch, seq, d = x_prompt.shape
    dec_batch, t_new, _ = x_sample.shape
    depth = norm_gain.shape[0]
    d_ff = w_ffn_down.shape[2]
    n_heads = attn_logit_bias.shape[1]
    assert d == n_heads * HEAD_DIM and d % LANES == 0 and d_ff % FF_CHUNK == 0
    n_mixers = 2

    xp = x_prompt.reshape(batch * seq, d)
    xs = x_sample.reshape(dec_batch * t_new, d)
    tm_p = _row_tile(batch * seq, 512)
    tm_s = _row_tile(dec_batch * t_new, 256)
    tq = _row_tile(seq, 256)
    assert seq % tm_p == 0

    wup, wdn = w_ffn_up.astype(BF16), w_ffn_down.astype(BF16)
    wqkv, wcin = w_attn_qkv.astype(BF16), w_conv_in.astype(BF16)
    wkt = wqkv[:, :, d:2 * d].transpose(0, 2, 1)
    wao = w_attn_out.astype(BF16)
    wco = w_conv_out.astype(BF16)
    to_feature_major = lambda pool: pool.transpose(0, 1, 3, 4, 2).reshape(pool.shape[:2] + (d, pool.shape[2]))
    pool_kt = to_feature_major(cache_k)
    pool_vt = to_feature_major(cache_v)

    outs = {name: [] for name in ("kp", "vp", "cp", "ks", "vs", "cs")}
    for i in range(depth):
        g = norm_gain[i]
        ffn = lambda x, half, tm, mix=None: _ffn_half(x, g[4 * half:4 * half + 2], wup[i, half], wdn[i, half], tm, mix)
        xp = ffn(xp, 0, tm_p)
        xs = ffn(xs, 0, tm_s)
        g_pre, g_post = g[2:3], g[3:4]
        if i % n_mixers == 0:
            a = i // n_mixers
            bias = attn_logit_bias[a].astype(F32) * LOG2E
            n_s = dec_batch * t_new
            qp, kpf, vpf, kpb, vpb = _attn_pre(xp, g_pre, wqkv[a], wkt[a], tm_p, batch, seq)
            qs, ksf, vsf, ksb, vsb = _attn_pre(xs, g_pre, wqkv[a], wkt[a], tm_s, 1, n_s)
            op = _prompt_attention(qp, kpb, vpb, bias, batch, seq, tq)
            per_seq_t = lambda xt: xt.reshape(d, dec_batch, t_new).transpose(1, 0, 2)
            os_ = _sample_attention(qs, per_seq_t(ksb), vsb, bias, pool_kt, pool_vt, page_table, a,
                                    dec_batch, t_new, pps=min(PAGES_PER_STEP, page_table.shape[1]))
            xp = ffn(xp, 1, tm_p, (op, g_post, wao[a]))
            xs = ffn(_mix_post(os_, xs, g_post, wao[a], tm_s), 1, tm_s)
            heads_last = lambda xt, b, t: xt.reshape(b, n_heads, HEAD_DIM, t).transpose(0, 3, 1, 2)
            outs["kp"].append(heads_last(kpf, batch, seq)); outs["vp"].append(heads_last(vpf, batch, seq))
            outs["ks"].append(heads_last(per_seq_t(ksf), dec_batch, t_new))
            outs["vs"].append(heads_last(per_seq_t(vsf), dec_batch, t_new))
        else:
            c = i // n_mixers
            n_state = state_conv.shape[2]
            assert n_state == conv_w.shape[1] - 1 == 2 and seq >= n_state and t_new >= n_state and t_new == HALO_ROWS
            us_, bs = _conv_pre(xs, g_pre, wcin[c], tm_s)
            halo_s = jnp.pad(state_conv[c], ((0, 0), (HALO_ROWS - n_state, 0), (0, 0))).reshape(-1, d)
            xs = _conv_post(us_, halo_s, bs, xs, conv_w[c], g_post, wco[c], t_new)
            outs["cs"].append(us_.reshape(dec_batch, t_new, d)[:, t_new - n_state:])
            xp, tails = _conv_ffn_half(xp, g[4:6], wup[i, 1], wdn[i, 1], g_pre, wcin[c], conv_w[c], g_post, wco[c],
                                       tm_p, seq)
            last_tiles = tails.reshape(batch, seq // tm_p, HALO_ROWS, d)[:, -1]
            outs["cp"].append(last_tiles[:, HALO_ROWS - n_state:])
            xs = ffn(xs, 1, tm_s)

    return (xp.reshape(batch, seq, d), xs.reshape(dec_batch, t_new, d),
            jnp.stack(outs["kp"]), jnp.stack(outs["vp"]), jnp.stack(outs["cp"]),
            jnp.stack(outs["ks"]), jnp.stack(outs["vs"]), jnp.stack(outs["cs"]))
```
